```python
import jax, jax.numpy as jnp
from jax import lax
import numpy as np

D_MODEL = 1024
BATCH = 8
SEQ = 4096
DEPTH = 2

GRID_W = 64
CTX_LEN = 256

N_BRANCH = 4
BRANCH_W = D_MODEL // N_BRANCH
NA_HEADS = 4
NA_HEAD_DIM = BRANCH_W // NA_HEADS
NA_WIN_H = 8
NA_WIN_W = 16
GLA_HEADS = 4
GLA_DV = BRANCH_W // GLA_HEADS
GLA_DK = GLA_DV // 2
GLA_GATE_RANK = 16
GLA_TAU = 16.0
GLA_CHUNK = 64
ROPE_BASE = 100.0
CONV_CH = BRANCH_W
CONV_WIDTH = 31
SGU_CH = BRANCH_W
SGU_GROUPS = 4
SGU_CHUNK = 128
PEER_HEADS = 8
PEER_NKEYS = 128
PEER_N_EXPERTS = PEER_NKEYS * PEER_NKEYS
PEER_DQ = 256
PEER_TOPK = 16
PEER_TOK_BLOCK = 128
ALPHA = (2 * DEPTH) ** 0.25
BETA = (8 * DEPTH) ** -0.25
NEG_INF = -1e30
LN_EPS = 1e-6

W_IN_SIZES = (3 * BRANCH_W, GLA_HEADS * GLA_DK, GLA_HEADS * GLA_DK, GLA_HEADS * GLA_DV, GLA_HEADS * GLA_DV,
              2 * GLA_GATE_RANK, 2 * CONV_CH, 2 * SGU_CH, N_BRANCH * D_MODEL)
W_IN_COLS = sum(W_IN_SIZES)
W_IN_SPLITS = tuple(int(s) for s in np.cumsum(W_IN_SIZES)[:-1])

kernel_name = "hybrid_dit_natten_gla_conformer_sgu_peer"


def layer_norm(x, g, b):
    xf = x.astype(jnp.float32)
    mu = xf.mean(-1, keepdims=True)
    var = jnp.square(xf - mu).mean(-1, keepdims=True)
    return ((xf - mu) * lax.rsqrt(var + LN_EPS)).astype(x.dtype) * g + b


def axial_rope(x, row, col):
    half = x.shape[-1] // 2
    nf = half // 2
    inv = 1.0 / (ROPE_BASE ** (jnp.arange(nf, dtype=jnp.float32) / nf))

    def rot(xa, pos):
        ang = pos.astype(jnp.float32)[:, None] * inv[None, :]
        cos = jnp.cos(ang)[None, :, None, :].astype(x.dtype)
        sin = jnp.sin(ang)[None, :, None, :].astype(x.dtype)
        x1, x2 = xa[..., :nf], xa[..., nf:]
        return jnp.concatenate([x1 * cos - x2 * sin, x1 * sin + x2 * cos], axis=-1)

    return jnp.concatenate([rot(x[..., :half], row), rot(x[..., half:], col)], axis=-1)


def neighbourhood_attention(qkv, qkv_c, rpb, need_ctx):
    B, T, _ = qkv.shape
    L = qkv_c.shape[1]
    rows = T // GRID_W
    kh = min(NA_WIN_H, rows)
    H, dh = NA_HEADS, NA_HEAD_DIM
    scale = dh ** -0.5
    q, k, v = [t.reshape(B, rows, GRID_W, H, dh) for t in jnp.split(qkv, 3, axis=-1)]
    qc, kc, vc = [t.reshape(B, L, H, dh) for t in jnp.split(qkv_c, 3, axis=-1)]
    r = jnp.arange(rows)
    r0 = jnp.clip(r - kh // 2, 0, rows - kh)
    key_rows = r0[:, None] + jnp.arange(kh)[None, :]
    kb = k[:, key_rows]
    vb = v[:, key_rows]
    colv = jnp.arange(GRID_W)
    c0 = jnp.clip(colv - NA_WIN_W // 2, 0, GRID_W - NA_WIN_W)
    in_win = (colv[None, :] >= c0[:, None]) & (colv[None, :] < c0[:, None] + NA_WIN_W)
    dr = key_rows - r[:, None] + NA_WIN_H - 1
    dc = jnp.clip(colv[None, :] - colv[:, None], -(NA_WIN_W - 1), NA_WIN_W - 1) + NA_WIN_W - 1
    bias = rpb[:, dr[:, None, :, None], dc[None, :, None, :]]
    s_win = jnp.einsum('brqhd,brkwhd->bhrqkw', q, kb).astype(jnp.float32) * scale + bias[None].astype(jnp.float32)
    s_win = jnp.where(in_win[:, None, :], s_win, NEG_INF)
    s_ctx = jnp.einsum('brqhd,blhd->bhrql', q, kc).astype(jnp.float32) * scale
    nwin = kh * GRID_W
    p = jax.nn.softmax(jnp.concatenate([s_win.reshape(B, H, rows, GRID_W, nwin), s_ctx], axis=-1), axis=-1).astype(v.dtype)
    p_win = p[..., :nwin].reshape(B, H, rows, GRID_W, kh, GRID_W)
    p_ctx = p[..., nwin:]
    o = jnp.einsum('bhrqkw,brkwhd->brqhd', p_win, vb) + jnp.einsum('bhrql,blhd->brqhd', p_ctx, vc)
    y = o.reshape(B, T, H * dh)
    if need_ctx:
        s_cc = jnp.einsum('blhd,bmhd->bhlm', qc, kc).astype(jnp.float32) * scale
        oc = jnp.einsum('bhlm,bmhd->blhd', jax.nn.softmax(s_cc, axis=-1).astype(vc.dtype), vc)
        yc = oc.reshape(B, L, H * dh)
    else:
        yc = None
    return y, yc


def gla_chunked(q, k, v, log_a, s0):
    B, T, H, dk = q.shape
    C = GLA_CHUNK
    N = T // C

    def chunks(t):
        return t.astype(jnp.float32).reshape(B, N, C, H, t.shape[-1]).transpose(0, 3, 1, 2, 4)

    qf, kf, vf, la = chunks(q), chunks(k), chunks(v), chunks(log_a)
    b = jnp.cumsum(la, axis=3)
    b_last = b[:, :, :, -1:]
    q_s = qf * jnp.exp(b)
    k_s = kf * jnp.exp(-b)
    k_end = kf * jnp.exp(b_last - b)
    lower = jnp.tril(jnp.ones((C, C), dtype=bool))
    a_intra = jnp.where(lower, jnp.einsum('bhncd,bhnsd->bhncs', q_s, k_s), 0.0)
    o_intra = jnp.einsum('bhncs,bhnse->bhnce', a_intra, vf)
    u = jnp.einsum('bhncd,bhnce->bhnde', k_end, vf)
    decay = jnp.exp(b_last[:, :, :, 0])

    def step(s, inp):
        d, un = inp
        return d[..., None] * s + un, s

    s_final, s_prev = lax.scan(step, s0, (decay.transpose(2, 0, 1, 3), u.transpose(2, 0, 1, 3, 4)))
    s_prev = s_prev.transpose(1, 2, 0, 3, 4)
    o = o_intra + jnp.einsum('bhncd,bhnde->bhnce', q_s, s_prev)
    return o.transpose(0, 2, 3, 1, 4).reshape(B, T, H, v.shape[-1]), s_final


def _gla_prep(q, k, v, lo, gate_up, gate_b):
    B, T, _ = q.shape
    q = q.reshape(B, T, GLA_HEADS, GLA_DK) * GLA_DK ** -0.5
    k = k.reshape(B, T, GLA_HEADS, GLA_DK)
    v = v.reshape(B, T, GLA_HEADS, GLA_DV)
    lo_f, lo_b = jnp.split(lo, 2, axis=-1)

    def log_gate(l, d):
        logits = (l @ gate_up[d] + gate_b[d]).astype(jnp.float32)
        return (jax.nn.log_sigmoid(logits) / GLA_TAU).reshape(B, T, GLA_HEADS, GLA_DK)

    return q, k, v, log_gate(lo_f, 0), log_gate(lo_b, 1)


def _gla_out(o, r, norm_g):
    B, T = o.shape[:2]
    o = o * lax.rsqrt(jnp.mean(jnp.square(o), axis=-1, keepdims=True) + LN_EPS) * norm_g
    return o.reshape(B, T, GLA_HEADS * GLA_DV).astype(r.dtype) * jax.nn.silu(r)


def gla_branch(zl, zc, gate_up, gate_b, norm_g, row, col, need_ctx):
    ql, kl, vl, lfl, lbl = _gla_prep(zl[0], zl[1], zl[2], zl[4], gate_up, gate_b)
    ql, kl = axial_rope(ql, row, col), axial_rope(kl, row, col)
    qc, kc, vc, lfc, lbc = _gla_prep(zc[0], zc[1], zc[2], zc[4], gate_up, gate_b)
    s0 = jnp.zeros((ql.shape[0], GLA_HEADS, GLA_DK, GLA_DV), jnp.float32)

    def flip(t):
        return jnp.flip(t, axis=1)

    oc_f, sc_f = gla_chunked(qc, kc, vc, lfc, s0)
    oc_b, sc_b = gla_chunked(flip(qc), flip(kc), flip(vc), flip(lbc), s0)
    ol_f, _ = gla_chunked(ql, kl, vl, lfl, sc_f)
    ol_b, _ = gla_chunked(flip(ql), flip(kl), flip(vl), flip(lbl), sc_b)
    y = _gla_out(ol_f + flip(ol_b), zl[3], norm_g)
    yc = _gla_out(oc_f + flip(oc_b), zc[3], norm_g) if need_ctx else None
    return y, yc


def conformer_conv(z, dw, b, ln_g, ln_b):
    a, g = jnp.split(z, 2, axis=-1)
    y = a * jax.nn.sigmoid(g)
    y = lax.conv_general_dilated(y, dw[:, None, :], window_strides=(1,),
                                 padding=[(CONV_WIDTH // 2, CONV_WIDTH // 2)],
                                 dimension_numbers=('NWC', 'WIO', 'NWC'),
                                 feature_group_count=CONV_CH) + b
    return jax.nn.silu(layer_norm(y, ln_g, ln_b))


def spatial_gating(z, ln_g, ln_b, ws, bs):
    z = jax.nn.gelu(z)
    u, v = jnp.split(z, 2, axis=-1)
    v = layer_norm(v, ln_g, ln_b)
    B, T, _ = v.shape
    vb = v.reshape(B, T // SGU_CHUNK, SGU_CHUNK, SGU_GROUPS, SGU_CH // SGU_GROUPS)
    s = jnp.einsum('gpq,bnqgc->bnpgc', ws, vb) + bs.T[None, None, :, :, None]
    return u * s.reshape(B, T, SGU_CH)


def merge_branches(ys, gates, w_branch, w_out):
    merged = None
    for i in range(N_BRANCH):
        term = jax.nn.sigmoid(gates[..., i * D_MODEL:(i + 1) * D_MODEL]) * (ys[i] @ w_branch[i])
        merged = term if merged is None else merged + term
    return merged @ w_out


def token_mixer(h, hc, row, col, need_ctx, w_in, na_rpb, gla_gate_up, gla_gate_b, gla_norm_g,
                conv_dw, conv_b, conv_ln_g, conv_ln_b, sgu_ln_g, sgu_ln_b, sgu_ws, sgu_bs, w_branch, w_out):
    z = jnp.split(h @ w_in, W_IN_SPLITS, axis=-1)
    zc = jnp.split(hc @ w_in, W_IN_SPLITS, axis=-1)
    y_na, yc_na = neighbourhood_attention(z[0], zc[0], na_rpb, need_ctx)
    y_gla, yc_gla = gla_branch(z[1:6], zc[1:6], gla_gate_up, gla_gate_b, gla_norm_g, row, col, need_ctx)
    y_cv = conformer_conv(z[6], conv_dw, conv_b, conv_ln_g, conv_ln_b)
    y_sg = spatial_gating(z[7], sgu_ln_g, sgu_ln_b, sgu_ws, sgu_bs)
    y = merge_branches((y_na, y_gla, y_cv, y_sg), z[8], w_branch, w_out)
    if need_ctx:
        yc_cv = conformer_conv(zc[6], conv_dw, conv_b, conv_ln_g, conv_ln_b)
        yc_sg = spatial_gating(zc[7], sgu_ln_g, sgu_ln_b, sgu_ws, sgu_bs)
        yc = merge_branches((yc_na, yc_gla, yc_cv, yc_sg), zc[8], w_branch, w_out)
    else:
        yc = None
    return y, yc


def peer_ffn(h, wq, keys, u_tab, v_tab):
    B, T, D = h.shape
    tok = h.reshape(-1, PEER_TOK_BLOCK, D)
    K = PEER_TOPK

    def block(xb):
        P = xb.shape[0]
        q = (xb @ wq).reshape(P, PEER_HEADS, 2, PEER_DQ // 2)
        s = jnp.einsum('phsd,hskd->phsk', q, keys).astype(jnp.float32)
        top_s, top_i = lax.top_k(s, K)
        cand_s = (top_s[:, :, 0, :, None] + top_s[:, :, 1, None, :]).reshape(P, PEER_HEADS, K * K)
        cand_i = (top_i[:, :, 0, :, None] * PEER_NKEYS + top_i[:, :, 1, None, :]).reshape(P, PEER_HEADS, K * K)
        best_s, best_j = lax.top_k(cand_s, K)
        idx = jnp.take_along_axis(cand_i, best_j, axis=-1)
        g = jax.nn.softmax(best_s, axis=-1).astype(xb.dtype)
        a = jax.nn.gelu(jnp.einsum('pd,phkd->phk', xb, u_tab[idx]))
        return jnp.einsum('phk,phkd->pd', g * a, v_tab[idx])

    return lax.map(block, tok).reshape(B, T, D)


def setup_inputs(seed: int = 0) -> dict:
    key = jax.random.key(seed)
    ks = iter(jax.random.split(key, 40))
    D, L = D_MODEL, DEPTH

    def nrm(shape, scale):
        return jax.random.normal(next(ks), shape, jnp.float32) * scale

    return {
        "x": nrm((BATCH, SEQ, D), 1.0),
        "c": nrm((BATCH, D), 1.0),
        "ctx": nrm((BATCH, CTX_LEN, D), 1.0),
        "c_ctx": nrm((D,), 1.0),
        "ada_w": nrm((L, D, 6 * D), 0.5 * D ** -0.5),
        "ada_b": nrm((L, 6 * D), 0.02),
        "w_in": nrm((L, D, W_IN_COLS), D ** -0.5),
        "na_rpb": nrm((L, NA_HEADS, 2 * NA_WIN_H - 1, 2 * NA_WIN_W - 1), 0.1),
        "gla_gate_up": nrm((L, 2, GLA_GATE_RANK, GLA_HEADS * GLA_DK), GLA_GATE_RANK ** -0.5),
        "gla_gate_b": nrm((L, 2, GLA_HEADS * GLA_DK), 0.1),
        "gla_norm_g": 1.0 + nrm((L, GLA_HEADS, GLA_DV), 0.02),
        "conv_dw": nrm((L, CONV_WIDTH, CONV_CH), CONV_WIDTH ** -0.5),
        "conv_b": nrm((L, CONV_CH), 0.02),
        "conv_ln_g": 1.0 + nrm((L, CONV_CH), 0.02),
        "conv_ln_b": nrm((L, CONV_CH), 0.02),
        "sgu_ln_g": 1.0 + nrm((L, SGU_CH), 0.02),
        "sgu_ln_b": nrm((L, SGU_CH), 0.02),
        "sgu_ws": nrm((L, SGU_GROUPS, SGU_CHUNK, SGU_CHUNK), 0.5 * SGU_CHUNK ** -0.5),
        "sgu_bs": 1.0 + nrm((L, SGU_GROUPS, SGU_CHUNK), 0.02),
        "w_branch": nrm((L, N_BRANCH, BRANCH_W, D), BRANCH_W ** -0.5),
        "w_out": nrm((L, D, D), BETA * D ** -0.5),
        "ln1_g": 1.0 + nrm((L, D), 0.02),
        "ln1_b": nrm((L, D), 0.02),
        "peer_wq": nrm((L, D, PEER_HEADS * PEER_DQ), D ** -0.5),
        "peer_keys": nrm((L, PEER_HEADS, 2, PEER_NKEYS, PEER_DQ // 2), (PEER_DQ // 2) ** -0.5),
        "peer_u": nrm((L, PEER_N_EXPERTS, D), D ** -0.5),
        "peer_v": nrm((L, PEER_N_EXPERTS, D), BETA),
        "ln2_g": 1.0 + nrm((L, D), 0.02),
        "ln2_b": nrm((L, D), 0.02),
    }


def reference(x, c, ctx, c_ctx, ada_w, ada_b, w_in, na_rpb, gla_gate_up, gla_gate_b, gla_norm_g,
              conv_dw, conv_b, conv_ln_g, conv_ln_b, sgu_ln_g, sgu_ln_b, sgu_ws, sgu_bs, w_branch, w_out,
              ln1_g, ln1_b, peer_wq, peer_keys, peer_u, peer_v, ln2_g, ln2_b):
    T = x.shape[1]
    t = jnp.arange(T)
    row, col = t // GRID_W, t % GRID_W
    c_silu = jax.nn.silu(c)
    cc_silu = jax.nn.silu(c_ctx)
    xc = ctx
    for l in range(DEPTH):
        need_ctx = l < DEPTH - 1
        mod = c_silu @ ada_w[l] + ada_b[l]
        mod_c = cc_silu @ ada_w[l] + ada_b[l]
        sh1, sc1, g1, sh2, sc2, g2 = [m[:, None, :] for m in jnp.split(mod, 6, axis=-1)]
        sh1c, sc1c, g1c, sh2c, sc2c, g2c = jnp.split(mod_c, 6, axis=-1)
        h = x * (1 + sc1) + sh1
        hc = xc * (1 + sc1c) + sh1c
        y, yc = token_mixer(h, hc, row, col, need_ctx, w_in[l], na_rpb[l], gla_gate_up[l], gla_gate_b[l],
                            gla_norm_g[l], conv_dw[l], conv_b[l], conv_ln_g[l], conv_ln_b[l], sgu_ln_g[l],
                            sgu_ln_b[l], sgu_ws[l], sgu_bs[l], w_branch[l], w_out[l])
        x = layer_norm(ALPHA * x + g1 * y, ln1_g[l], ln1_b[l])
        h2 = x * (1 + sc2) + sh2
        x = layer_norm(ALPHA * x + g2 * peer_ffn(h2, peer_wq[l], peer_keys[l], peer_u[l], peer_v[l]), ln2_g[l], ln2_b[l])
        if need_ctx:
            xc = layer_norm(ALPHA * xc + g1c * yc, ln1_g[l], ln1_b[l])
            h2c = xc * (1 + sc2c) + sh2c
            xc = layer_norm(ALPHA * xc + g2c * peer_ffn(h2c, peer_wq[l], peer_keys[l], peer_u[l], peer_v[l]), ln2_g[l], ln2_b[l])
    return x
```

```python
import functools

import numpy as np
import jax
import jax.numpy as jnp
from jax import lax
from jax.experimental import pallas as pl
from jax.experimental.pallas import tpu as pltpu

F32 = jnp.float32
BF16 = jnp.bfloat16
I32 = jnp.int32

D_MODEL = 1024
GRID_W = 64
BRANCH_W = 256
NA_HEADS = 4
NA_HEAD_DIM = 64
NA_WIN_H = 8
NA_WIN_W = 16
GLA_HEADS = 4
GLA_DV = 64
GLA_DK = 32
GLA_GATE_RANK = 16
GLA_TAU = 16.0
GLA_CHUNK = 64
ROPE_BASE = 100.0
CONV_WIDTH = 31
SGU_GROUPS = 4
SGU_CHUNK = 128
PEER_HEADS = 8
PEER_NKEYS = 128
PEER_DQ = 256
PEER_TOPK = 16
DEPTH = 2
ALPHA = (2 * DEPTH) ** 0.25
NEG_INF = -1e30
LN_EPS = 1e-6

SUBLANES = 8
LANES = 128

Z_GATES = 0
Z_CONV = 4096
Z_SGU = 4608
Z_GLA = 5120
Z_NA = 6144
Z_COLS = 7168

TOK_TILE = 512
GATHER_TILE = 128

_NT = (((1,), (1,)), ((), ()))
_TN = (((0,), (0,)), ((), ()))


def _cparams(sem, vmem_mb=48):
    return pltpu.CompilerParams(dimension_semantics=sem, vmem_limit_bytes=vmem_mb * 1024 * 1024)


def _split3(x):
    hi = x.astype(BF16)
    r1 = x - hi.astype(F32)
    mid = r1.astype(BF16)
    lo = (r1 - mid.astype(F32)).astype(BF16)
    return hi, mid, lo


def _dot_exact(a_bf16, x, dims=None, lhs_is_x=False):
    out = None
    for p in _split3(x):
        if dims is None:
            t = jnp.dot(p, a_bf16, preferred_element_type=F32) if lhs_is_x else jnp.dot(a_bf16, p, preferred_element_type=F32)
        else:
            t = lax.dot_general(p, a_bf16, dims, preferred_element_type=F32) if lhs_is_x else lax.dot_general(a_bf16, p, dims, preferred_element_type=F32)
        out = t if out is None else out + t
    return out


def _layer_norm(x, g, b):
    mu = jnp.mean(x, axis=-1, keepdims=True)
    xc = x - mu
    var = jnp.mean(xc * xc, axis=-1, keepdims=True)
    return xc * lax.rsqrt(var + LN_EPS) * g + b


ADA_TILE = 512


def _ada_body(c_ref, w_ref, b_ref, o_ref):
    c = c_ref[...]
    cs = c * jax.nn.sigmoid(c)
    o_ref[...] = jnp.dot(cs, w_ref[...], preferred_element_type=F32, precision=lax.Precision.HIGHEST) + b_ref[...]


def _ada(cpad, w, b):
    rows, d = cpad.shape
    n = w.shape[1]
    return pl.pallas_call(
        _ada_body, grid=(n // ADA_TILE,),
        in_specs=[pl.BlockSpec((rows, d), lambda j: (0, 0)),
                  pl.BlockSpec((d, ADA_TILE), lambda j: (0, j)),
                  pl.BlockSpec((1, ADA_TILE), lambda j: (0, j))],
        out_specs=pl.BlockSpec((rows, ADA_TILE), lambda j: (0, j)),
        out_shape=jax.ShapeDtypeStruct((rows, n), F32),
        compiler_params=_cparams(("arbitrary",)), name="ada",
    )(cpad, w, b)


def _modmm_body(mod_ref, x_ref, sc_ref, sh_ref, w_ref, o_ref):
    h = x_ref[...] * (1.0 + sc_ref[0]) + sh_ref[0]
    o_ref[...] = jnp.dot(h.astype(BF16), w_ref[...], preferred_element_type=F32)


def _mm_body(x_ref, w_ref, o_ref):
    o_ref[...] = jnp.dot(x_ref[...].astype(BF16), w_ref[...], preferred_element_type=F32)


def _modmm(x, sc, sh, w, mod_idx, n_tiles, tn):
    d = x.shape[1]
    n = w.shape[1]
    grid_spec = pltpu.PrefetchScalarGridSpec(
        num_scalar_prefetch=1,
        grid=(n // tn, n_tiles),
        in_specs=[
            pl.BlockSpec((TOK_TILE, d), lambda j, i, m: (i, 0)),
            pl.BlockSpec((1, 1, d), lambda j, i, m: (m[i], 0, 0)),
            pl.BlockSpec((1, 1, d), lambda j, i, m: (m[i], 0, 0)),
            pl.BlockSpec((d, tn), lambda j, i, m: (0, j)),
        ],
        out_specs=pl.BlockSpec((TOK_TILE, tn), lambda j, i, m: (i, j)),
    )
    return pl.pallas_call(
        _modmm_body, grid_spec=grid_spec,
        out_shape=jax.ShapeDtypeStruct((n_tiles * TOK_TILE, n), F32),
        compiler_params=_cparams(("arbitrary", "arbitrary")), name="modmm",
    )(mod_idx, x, sc, sh, w)


def _mm(x, w, n_tiles, tn):
    d = x.shape[1]
    n = w.shape[1]
    return pl.pallas_call(
        _mm_body, grid=(n // tn, n_tiles),
        in_specs=[pl.BlockSpec((TOK_TILE, d), lambda j, i: (i, 0)),
                  pl.BlockSpec((d, tn), lambda j, i: (0, j))],
        out_specs=pl.BlockSpec((TOK_TILE, tn), lambda j, i: (i, j)),
        out_shape=jax.ShapeDtypeStruct((n_tiles * TOK_TILE, n), F32),
        compiler_params=_cparams(("arbitrary", "arbitrary")), name="peer_q",
    )(x, w)


def _head_masks(width, per_head, heads):
    lane = lax.broadcasted_iota(I32, (1, width), 1)
    return [(lane >= h * per_head) & (lane < (h + 1) * per_head) for h in range(heads)]


def _na_body(q_ref, k_ref, v_ref, kc_ref, vc_ref, bias_ref, o_ref, *, rows):
    r = pl.program_id(1)
    r0 = jnp.clip(r - NA_WIN_H // 2, 0, rows - NA_WIN_H)
    delta = r - r0
    start = pl.multiple_of(r0 * GRID_W, GRID_W)
    nwin = NA_WIN_H * GRID_W
    kwin = k_ref[pl.ds(start, nwin), :].astype(BF16)
    vwin = v_ref[pl.ds(start, nwin), :].astype(BF16)
    kc = kc_ref[...].astype(BF16)
    vc = vc_ref[...].astype(BF16)
    q = q_ref[...] * (NA_HEAD_DIM ** -0.5)
    out = jnp.zeros(q.shape, F32)
    for h, hm in enumerate(_head_masks(BRANCH_W, NA_HEAD_DIM, NA_HEADS)):
        qh = jnp.where(hm, q, 0.0).astype(BF16)
        s = lax.dot_general(qh, kwin, _NT, preferred_element_type=F32) + bias_ref[h, delta]
        sc = lax.dot_general(qh, kc, _NT, preferred_element_type=F32)
        m = jnp.maximum(jnp.max(s, axis=1, keepdims=True), jnp.max(sc, axis=1, keepdims=True))
        e = jnp.exp(s - m)
        ec = jnp.exp(sc - m)
        den = jnp.sum(e, axis=1, keepdims=True) + jnp.sum(ec, axis=1, keepdims=True)
        oh = (jnp.dot(e.astype(BF16), vwin, preferred_element_type=F32)
              + jnp.dot(ec.astype(BF16), vc, preferred_element_type=F32))
        out = out + jnp.where(hm, oh / den, 0.0)
    o_ref[...] = out


def _na_bias_table(rpb):
    delta = np.arange(NA_WIN_H)[:, None]
    kh = np.arange(NA_WIN_H)[None, :]
    dr = kh - delta + NA_WIN_H - 1
    colv = np.arange(GRID_W)
    c0 = np.clip(colv - NA_WIN_W // 2, 0, GRID_W - NA_WIN_W)
    in_win = (colv[None, :] >= c0[:, None]) & (colv[None, :] < c0[:, None] + NA_WIN_W)
    dc = np.clip(colv[None, :] - colv[:, None], -(NA_WIN_W - 1), NA_WIN_W - 1) + NA_WIN_W - 1
    b = rpb[:, dr[:, None, :, None], dc[None, :, None, :]]
    b = jnp.where(jnp.asarray(in_win)[None, None, :, None, :], b, NEG_INF)
    return b.reshape(NA_HEADS, NA_WIN_H, GRID_W, NA_WIN_H * GRID_W).astype(F32)


def _na_attention(z, bias, B, T, L):
    rows = T // GRID_W
    cq, ck, cv = Z_NA // BRANCH_W, Z_NA // BRANCH_W + 1, Z_NA // BRANCH_W + 2
    ctx0 = (B * T) // L
    return pl.pallas_call(
        functools.partial(_na_body, rows=rows), grid=(B, rows),
        in_specs=[
            pl.BlockSpec((GRID_W, BRANCH_W), lambda b, r: (b * rows + r, cq)),
            pl.BlockSpec((T, BRANCH_W), lambda b, r: (b, ck)),
            pl.BlockSpec((T, BRANCH_W), lambda b, r: (b, cv)),
            pl.BlockSpec((L, BRANCH_W), lambda b, r: (ctx0 + b, ck)),
            pl.BlockSpec((L, BRANCH_W), lambda b, r: (ctx0 + b, cv)),
            pl.BlockSpec(bias.shape, lambda b, r: (0, 0, 0, 0)),
        ],
        out_specs=pl.BlockSpec((GRID_W, BRANCH_W), lambda b, r: (b * rows + r, 0)),
        out_shape=jax.ShapeDtypeStruct((B * T, BRANCH_W), F32),
        compiler_params=_cparams(("arbitrary", "arbitrary")), name="na_attn",
    )(z, z, z, z, z, bias)


def _ctx_attn_body(q_ref, k_ref, v_ref, o_ref):
    k = k_ref[...].astype(BF16)
    v = v_ref[...].astype(BF16)
    q = q_ref[...] * (NA_HEAD_DIM ** -0.5)
    out = jnp.zeros(q.shape, F32)
    for hm in _head_masks(BRANCH_W, NA_HEAD_DIM, NA_HEADS):
        qh = jnp.where(hm, q, 0.0).astype(BF16)
        s = lax.dot_general(qh, k, _NT, preferred_element_type=F32)
        m = jnp.max(s, axis=1, keepdims=True)
        e = jnp.exp(s - m)
        den = jnp.sum(e, axis=1, keepdims=True)
        oh = jnp.dot(e.astype(BF16), v, preferred_element_type=F32)
        out = out + jnp.where(hm, oh / den, 0.0)
    o_ref[...] = out


def _ctx_attention(z, B, T, L):
    cq = Z_NA // BRANCH_W
    ctx0 = (B * T) // L
    return pl.pallas_call(
        _ctx_attn_body, grid=(B,),
        in_specs=[pl.BlockSpec((L, BRANCH_W), lambda b, c=c: (ctx0 + b, cq + c)) for c in range(3)],
        out_specs=pl.BlockSpec((L, BRANCH_W), lambda b: (b, 0)),
        out_shape=jax.ShapeDtypeStruct((B * L, BRANCH_W), F32),
        compiler_params=_cparams(("arbitrary",)), name="ctx_attn",
    )(z, z, z)


def _gla_body(z_ref, cos_ref, sin_ref, gup_ref, gb_ref, s0_ref, o_ref, sfin_ref, state_ref, *, reverse, n_steps):
    g = pl.program_id(1)

    @pl.when(g == 0)
    def _():
        state_ref[...] = s0_ref[0]

    kw = GLA_HEADS * GLA_DK
    vw = GLA_HEADS * GLA_DV
    q = z_ref[:, 0:kw] * (GLA_DK ** -0.5)
    k = z_ref[:, kw:2 * kw]
    v = z_ref[:, 2 * kw:2 * kw + vw]
    lo = z_ref[:, 2 * kw + 2 * vw:2 * kw + 2 * vw + LANES]
    cos = cos_ref[...]
    sin = sin_ref[...]
    lane = lax.broadcasted_iota(I32, (1, kw), 1)
    first = (lane % (GLA_DK // 2)) < (GLA_DK // 4)

    def rope(x):
        partner = jnp.where(first, pltpu.roll(x, kw - GLA_DK // 4, 1), pltpu.roll(x, GLA_DK // 4, 1))
        return x * cos + partner * sin

    q = rope(q)
    k = rope(k)
    logits = jnp.dot(lo.astype(BF16), gup_ref[...], preferred_element_type=F32) + gb_ref[...]
    la = (jnp.minimum(logits, 0.0) - jnp.log1p(jnp.exp(-jnp.abs(logits)))) / GLA_TAU

    C = GLA_CHUNK
    ri = lax.broadcasted_iota(I32, (C, C), 0)
    ci = lax.broadcasted_iota(I32, (C, C), 1)
    tri = (ri <= ci) if reverse else (ri >= ci)
    cum = jnp.where(tri, 1.0, 0.0).astype(BF16)
    tri4 = jnp.concatenate([tri] * GLA_HEADS, axis=0)
    hm_k = _head_masks(kw, GLA_DK, GLA_HEADS)
    hm_v = _head_masks(vw, GLA_DV, GLA_HEADS)
    srow = lax.broadcasted_iota(I32, (kw, vw), 0) // GLA_DK
    scol = lax.broadcasted_iota(I32, (kw, vw), 1) // GLA_DV
    blockmask = srow == scol
    ones_cv = jnp.ones((C, vw), BF16)

    S = state_ref[...]
    n_chunks = z_ref.shape[0] // C
    order = range(n_chunks - 1, -1, -1) if reverse else range(n_chunks)
    for c in order:
        sl = slice(c * C, (c + 1) * C)
        la_c = la[sl]
        b = _dot_exact(cum, la_c)
        bl = b[0:1] if reverse else b[C - 1:C]
        qs = q[sl] * jnp.exp(b)
        ks = k[sl] * jnp.exp(-b)
        ke = k[sl] * jnp.exp(bl - b)
        vb = v[sl].astype(BF16)
        qs_b = qs.astype(BF16)
        qstack = jnp.concatenate([jnp.where(hm, qs, 0.0) for hm in hm_k], axis=0).astype(BF16)
        a = lax.dot_general(qstack, ks.astype(BF16), _NT, preferred_element_type=F32)
        a = jnp.where(tri4, a, 0.0)
        o_stack = jnp.dot(a.astype(BF16), vb, preferred_element_type=F32)
        o_c = jnp.dot(qs_b, S.astype(BF16), preferred_element_type=F32)
        for h, hm in enumerate(hm_v):
            o_c = o_c + jnp.where(hm, o_stack[h * C:(h + 1) * C], 0.0)
        o_ref[sl, :] = o_c
        u = lax.dot_general(ke.astype(BF16), vb, _TN, preferred_element_type=F32)
        dcol = _dot_exact(ones_cv, la_c, dims=_TN, lhs_is_x=True)
        S = jnp.exp(dcol) * S + jnp.where(blockmask, u, 0.0)
    state_ref[...] = S

    @pl.when(g == n_steps - 1)
    def _():
        sfin_ref[0] = S


def _gla_scan(z, cos, sin, gup, gb, s0, *, B, seq, row0_blocks, tile, reverse):
    n_steps = seq // tile
    kw = GLA_HEADS * GLA_DK
    vw = GLA_HEADS * GLA_DV

    def step(g):
        return n_steps - 1 - g if reverse else g

    return pl.pallas_call(
        functools.partial(_gla_body, reverse=reverse, n_steps=n_steps), grid=(B, n_steps),
        in_specs=[
            pl.BlockSpec((tile, 1024), lambda b, g: (row0_blocks + b * n_steps + step(g), Z_GLA // 1024)),
            pl.BlockSpec((tile, kw), lambda b, g: (step(g), 0)),
            pl.BlockSpec((tile, kw), lambda b, g: (step(g), 0)),
            pl.BlockSpec((LANES, kw), lambda b, g: (0, 0)),
            pl.BlockSpec((1, kw), lambda b, g: (0, 0)),
            pl.BlockSpec((1, kw, vw), lambda b, g: (b, 0, 0)),
        ],
        out_specs=[
            pl.BlockSpec((tile, vw), lambda b, g: (b * n_steps + step(g), 0)),
            pl.BlockSpec((1, kw, vw), lambda b, g: (b, 0, 0)),
        ],
        out_shape=[jax.ShapeDtypeStruct((B * seq, vw), F32), jax.ShapeDtypeStruct((B, kw, vw), F32)],
        scratch_shapes=[pltpu.VMEM((kw, vw), F32)],
        compiler_params=_cparams(("arbitrary", "arbitrary")), name="gla_rev" if reverse else "gla_fwd",
    )(z, cos, sin, gup, gb, s0)


def _rope_tables(T):
    t = np.arange(T)
    row, col = t // GRID_W, t % GRID_W
    nf = GLA_DK // 4
    inv = 1.0 / (ROPE_BASE ** (jnp.arange(nf, dtype=F32) / nf))
    j = np.arange(GLA_HEADS * GLA_DK)
    d = j % GLA_DK
    use_col = (d // (GLA_DK // 2)) == 1
    e = d % (GLA_DK // 2)
    fi = e % nf
    pos = jnp.where(jnp.asarray(use_col)[None, :], jnp.asarray(col, F32)[:, None], jnp.asarray(row, F32)[:, None])
    ang = pos * inv[fi][None, :]
    sign = jnp.asarray(np.where(e < nf, -1.0, 1.0), F32)[None, :]
    return jnp.cos(ang), jnp.sin(ang) * sign


CONV_PAD = 16
CONV_ROWS = 128


def _conv_body(z_ref, dw_ref, b_ref, g_ref, be_ref, o_ref, ypad_ref, *, seq):
    zeros = jnp.zeros((CONV_PAD, BRANCH_W), F32)
    ypad_ref[0:CONV_PAD, :] = zeros
    ypad_ref[seq + CONV_PAD:seq + 2 * CONV_PAD, :] = zeros

    def glu(i, carry):
        base = pl.multiple_of(i * CONV_ROWS, CONV_ROWS)
        a = z_ref[pl.ds(base, CONV_ROWS), 0:BRANCH_W]
        gate = z_ref[pl.ds(base, CONV_ROWS), BRANCH_W:2 * BRANCH_W]
        ypad_ref[pl.ds(base + CONV_PAD, CONV_ROWS), :] = a * jax.nn.sigmoid(gate)
        return carry

    lax.fori_loop(0, seq // CONV_ROWS, glu, 0)

    def tile(i, carry):
        base = pl.multiple_of(i * CONV_ROWS, CONV_ROWS)
        acc = jnp.zeros((CONV_ROWS, BRANCH_W), F32)
        win = ypad_ref[pl.ds(base, CONV_ROWS + 2 * CONV_PAD), :]
        for j in range(CONV_WIDTH):
            off = CONV_PAD - CONV_WIDTH // 2 + j
            acc = acc + win[off:off + CONV_ROWS] * dw_ref[j:j + 1, :]
        y = _layer_norm(acc + b_ref[...], g_ref[...], be_ref[...])
        o_ref[pl.ds(base, CONV_ROWS), :] = y * jax.nn.sigmoid(y)
        return carry

    lax.fori_loop(0, seq // CONV_ROWS, tile, 0)


def _conformer_conv(z, dw, b, g, be, *, B, seq, row0_blocks):
    vec = pl.BlockSpec((1, BRANCH_W), lambda i: (0, 0))
    return pl.pallas_call(
        functools.partial(_conv_body, seq=seq), grid=(B,),
        in_specs=[pl.BlockSpec((seq, 2 * BRANCH_W), lambda i: (row0_blocks + i, Z_CONV // (2 * BRANCH_W))),
                  pl.BlockSpec((CONV_WIDTH, BRANCH_W), lambda i: (0, 0)), vec, vec, vec],
        out_specs=pl.BlockSpec((seq, BRANCH_W), lambda i: (i, 0)),
        out_shape=jax.ShapeDtypeStruct((B * seq, BRANCH_W), F32),
        scratch_shapes=[pltpu.VMEM((seq + 2 * CONV_PAD, BRANCH_W), F32)],
        compiler_params=_cparams(("arbitrary",)), name="conformer_conv",
    )(z, dw, b, g, be)


def _sgu_body(z_ref, g_ref, b_ref, ws_ref, bs_ref, o_ref):
    zz = jax.nn.gelu(z_ref[...])
    u = zz[:, 0:BRANCH_W]
    v = _layer_norm(zz[:, BRANCH_W:2 * BRANCH_W], g_ref[...], b_ref[...])
    gms = _head_masks(BRANCH_W, BRANCH_W // SGU_GROUPS, SGU_GROUPS)
    for c in range(z_ref.shape[0] // SGU_CHUNK):
        sl = slice(c * SGU_CHUNK, (c + 1) * SGU_CHUNK)
        vb = v[sl].astype(BF16)
        s = bs_ref[...]
        for gi, gm in enumerate(gms):
            s = s + jnp.where(gm, jnp.dot(ws_ref[gi], vb, preferred_element_type=F32), 0.0)
        o_ref[sl, :] = u[sl] * s


def _spatial_gating(z, g, b, ws, bs_exp, n_tiles):
    vec = pl.BlockSpec((1, BRANCH_W), lambda i: (0, 0))
    return pl.pallas_call(
        _sgu_body, grid=(n_tiles,),
        in_specs=[pl.BlockSpec((TOK_TILE, 2 * BRANCH_W), lambda i: (i, Z_SGU // (2 * BRANCH_W))), vec, vec,
                  pl.BlockSpec(ws.shape, lambda i: (0, 0, 0)),
                  pl.BlockSpec((SGU_CHUNK, BRANCH_W), lambda i: (0, 0))],
        out_specs=pl.BlockSpec((TOK_TILE, BRANCH_W), lambda i: (i, 0)),
        out_shape=jax.ShapeDtypeStruct((n_tiles * TOK_TILE, BRANCH_W), F32),
        compiler_params=_cparams(("arbitrary",)), name="sgu",
    )(z, g, b, ws, bs_exp)


MERGE_TILE = 256


def _merge_body(mod_ref, gates_ref, yna_ref, of_ref, ob_ref, r_ref, ycv_ref, ysg_ref, x_ref, g1_ref, sc2_ref, sh2_ref,
                ng_ref, wb_ref, wo_ref, lg_ref, lb_ref, x1_ref, h2_ref):
    o = of_ref[...] + ob_ref[...]
    vw = GLA_HEADS * GLA_DV
    hr = lax.broadcasted_iota(I32, (vw, vw), 0) // GLA_DV
    hc = lax.broadcasted_iota(I32, (vw, vw), 1) // GLA_DV
    same_head = jnp.where(hr == hc, 1.0, 0.0).astype(BF16)
    ms = _dot_exact(same_head, o * o, lhs_is_x=True) * (1.0 / GLA_DV)
    r = r_ref[...]
    y_gla = o * lax.rsqrt(ms + LN_EPS) * ng_ref[...] * (r * jax.nn.sigmoid(r))
    ys = (yna_ref[...], y_gla, ycv_ref[...], ysg_ref[...])
    merged = None
    for i in range(4):
        gate = jax.nn.sigmoid(gates_ref[:, i * D_MODEL:(i + 1) * D_MODEL])
        term = gate * jnp.dot(ys[i].astype(BF16), wb_ref[i], preferred_element_type=F32)
        merged = term if merged is None else merged + term
    y = jnp.dot(merged.astype(BF16), wo_ref[...], preferred_element_type=F32)
    x1 = _layer_norm(ALPHA * x_ref[...] + g1_ref[0] * y, lg_ref[...], lb_ref[...])
    x1_ref[...] = x1
    h2_ref[...] = x1 * (1.0 + sc2_ref[0]) + sh2_ref[0]


def _merge(z, yna, of, ob, ycv, ysg, x, g1, sc2, sh2, ng, wb, wo, lg, lb, mod_idx, n_tiles):
    d = D_MODEL
    br = pl.BlockSpec((MERGE_TILE, BRANCH_W), lambda i, m: (i, 0))
    modv = pl.BlockSpec((1, 1, d), lambda i, m: (m[i], 0, 0))
    vec = pl.BlockSpec((1, d), lambda i, m: (0, 0))
    xt = pl.BlockSpec((MERGE_TILE, d), lambda i, m: (i, 0))
    grid_spec = pltpu.PrefetchScalarGridSpec(
        num_scalar_prefetch=1, grid=(n_tiles,),
        in_specs=[
            pl.BlockSpec((MERGE_TILE, 4 * d), lambda i, m: (i, 0)),
            br, br, br,
            pl.BlockSpec((MERGE_TILE, BRANCH_W), lambda i, m: (i, (Z_GLA + 512) // BRANCH_W)),
            br, br, xt, modv, modv, modv,
            pl.BlockSpec((1, BRANCH_W), lambda i, m: (0, 0)),
            pl.BlockSpec(wb.shape, lambda i, m: (0, 0, 0)),
            pl.BlockSpec(wo.shape, lambda i, m: (0, 0)),
            vec, vec,
        ],
        out_specs=[xt, xt],
    )
    return pl.pallas_call(
        _merge_body, grid_spec=grid_spec,
        out_shape=[jax.ShapeDtypeStruct((n_tiles * MERGE_TILE, d), F32)] * 2,
        compiler_params=_cparams(("arbitrary",)), name="merge",
    )(mod_idx, z, yna, of, ob, z, ycv, ysg, x, g1, sc2, sh2, ng, wb, wo, lg, lb)


TOPK_TILE = 256


def _top16(vals, payload):
    n_cand = vals.shape[0]
    pos_iota = lax.broadcasted_iota(I32, vals.shape, 0)
    out_v, out_p = [], []
    for _ in range(PEER_TOPK):
        m = jnp.max(vals, axis=0, keepdims=True)
        pos = jnp.min(jnp.where(vals == m, pos_iota, n_cand), axis=0, keepdims=True)
        hit = pos_iota == pos
        out_v.append(m)
        out_p.append(jnp.max(jnp.where(hit, payload, -1), axis=0, keepdims=True))
        vals = jnp.where(hit, -jnp.inf, vals)
    return jnp.concatenate(out_v, axis=0), jnp.concatenate(out_p, axis=0)


def _topk_body(q_ref, keys_ref, idx_ref, g_ref):
    half = PEER_DQ // 2
    key_iota = lax.broadcasted_iota(I32, (PEER_NKEYS, q_ref.shape[0]), 0)
    tops = []
    for s in range(2):
        qs = q_ref[:, s * half:(s + 1) * half].astype(BF16)
        sc = lax.dot_general(keys_ref[0, s], qs, _NT, preferred_element_type=F32)
        tops.append(_top16(sc, key_iota))
    (a, ia), (b, ib) = tops
    cand = jnp.concatenate([a[i:i + 1] + b for i in range(PEER_TOPK)], axis=0)
    code = jnp.concatenate([ia[i:i + 1] * PEER_NKEYS + ib for i in range(PEER_TOPK)], axis=0)
    best, idx = _top16(cand, code)
    e = jnp.exp(best - jnp.max(best, axis=0, keepdims=True))
    g_ref[0] = e / jnp.sum(e, axis=0, keepdims=True)
    idx_ref[0] = idx


def _peer_topk(q, keys, n_tiles):
    ntok = n_tiles * TOPK_TILE
    out = pl.BlockSpec((1, PEER_TOPK, TOPK_TILE), lambda i, h: (h, 0, i))
    return pl.pallas_call(
        _topk_body, grid=(n_tiles, PEER_HEADS),
        in_specs=[pl.BlockSpec((TOPK_TILE, PEER_DQ), lambda i, h: (i, h)),
                  pl.BlockSpec((1, 2, PEER_NKEYS, PEER_DQ // 2), lambda i, h: (h, 0, 0, 0))],
        out_specs=[out, out],
        out_shape=[jax.ShapeDtypeStruct((PEER_HEADS, PEER_TOPK, ntok), I32),
                   jax.ShapeDtypeStruct((PEER_HEADS, PEER_TOPK, ntok), F32)],
        compiler_params=_cparams(("arbitrary", "arbitrary")), name="peer_topk",
    )(q, keys)


HALF_SUB = SUBLANES // 2
N_SEL = PEER_HEADS * PEER_TOPK


def _pack_table(tab):
    bits = lax.bitcast_convert_type(tab.astype(BF16), jnp.uint16).astype(jnp.uint32)
    half = tab.shape[1] // 2
    words = bits[:, :half] | (bits[:, half:] << 16)
    return lax.bitcast_convert_type(words, I32).reshape(tab.shape[0], HALF_SUB, LANES)


def _unpack(words):
    lo = pltpu.bitcast(words << 16, F32)
    hi = pltpu.bitcast(words & jnp.int32(-65536), F32)
    return lo, hi


def _peer_u_body(idx_ref, h_ref, g_ref, tab_ref, w_ref, part_ref, a_ref):
    ones = jnp.ones((SUBLANES, LANES), BF16)

    def token(p, carry):
        x = h_ref[p]
        xlo = x[0:HALF_SUB]
        xhi = x[HALF_SUB:SUBLANES]
        for k in range(N_SEL):
            lo, hi = _unpack(tab_ref[idx_ref[p, k]])
            part_ref[pl.ds(k, 1), :] = jnp.sum(lo * xlo + hi * xhi, axis=0, keepdims=True)
        row = _dot_exact(ones, part_ref[...], dims=_NT)
        a_ref[pl.ds(p, 1), :] = row[0:1]
        return carry

    lax.fori_loop(0, h_ref.shape[0], token, 0)
    w_ref[...] = g_ref[...] * jax.nn.gelu(a_ref[...])


def _peer_v_body(idx_ref, w_ref, tab_ref, o_ref):
    n_acc = 4

    def token(p, carry):
        acc_lo = [jnp.zeros((HALF_SUB, LANES), F32) for _ in range(n_acc)]
        acc_hi = [jnp.zeros((HALF_SUB, LANES), F32) for _ in range(n_acc)]
        for k in range(N_SEL):
            lo, hi = _unpack(tab_ref[idx_ref[p, k]])
            wk = w_ref[p, k]
            acc_lo[k % n_acc] = acc_lo[k % n_acc] + wk * lo
            acc_hi[k % n_acc] = acc_hi[k % n_acc] + wk * hi
        lo = (acc_lo[0] + acc_lo[1]) + (acc_lo[2] + acc_lo[3])
        hi = (acc_hi[0] + acc_hi[1]) + (acc_hi[2] + acc_hi[3])
        o_ref[p] = jnp.concatenate([lo, hi], axis=0)
        return carry

    lax.fori_loop(0, o_ref.shape[0], token, 0)


def _table_spec(tab):
    return pl.BlockSpec(tab.shape, lambda i: (0, 0, 0), pipeline_mode=pl.Buffered(1))


def _peer_u(idx, h3, g, tab, n_tiles):
    tb = GATHER_TILE
    smem = pl.BlockSpec((tb, N_SEL), lambda i: (i, 0), memory_space=pltpu.SMEM)
    return pl.pallas_call(
        _peer_u_body, grid=(n_tiles,),
        in_specs=[smem, pl.BlockSpec((tb, SUBLANES, LANES), lambda i: (i, 0, 0)),
                  pl.BlockSpec((tb, N_SEL), lambda i: (i, 0)), _table_spec(tab)],
        out_specs=pl.BlockSpec((tb, N_SEL), lambda i: (i, 0)),
        out_shape=jax.ShapeDtypeStruct((n_tiles * tb, N_SEL), F32),
        scratch_shapes=[pltpu.VMEM((N_SEL, LANES), F32), pltpu.VMEM((tb, N_SEL), F32)],
        compiler_params=_cparams(("arbitrary",), vmem_mb=56), name="peer_u",
    )(idx, h3, g, tab)


def _peer_v(idx, w, tab, n_tiles):
    tb = GATHER_TILE
    smem = pl.BlockSpec((tb, N_SEL), lambda i: (i, 0), memory_space=pltpu.SMEM)
    return pl.pallas_call(
        _peer_v_body, grid=(n_tiles,),
        in_specs=[smem, smem, _table_spec(tab)],
        out_specs=pl.BlockSpec((tb, SUBLANES, LANES), lambda i: (i, 0, 0)),
        out_shape=jax.ShapeDtypeStruct((n_tiles * tb, SUBLANES, LANES), F32),
        compiler_params=_cparams(("arbitrary",), vmem_mb=56), name="peer_v",
    )(idx, w, tab)


def _ln2_body(mod_ref, x_ref, f_ref, g2_ref, lg_ref, lb_ref, o_ref):
    o_ref[...] = _layer_norm(ALPHA * x_ref[...] + g2_ref[0] * f_ref[...], lg_ref[...], lb_ref[...])


def _ln2(x1, ffn, g2, lg, lb, mod_idx, n_tiles):
    d = D_MODEL
    xt = pl.BlockSpec((TOK_TILE, d), lambda i, m: (i, 0))
    vec = pl.BlockSpec((1, d), lambda i, m: (0, 0))
    grid_spec = pltpu.PrefetchScalarGridSpec(
        num_scalar_prefetch=1, grid=(n_tiles,),
        in_specs=[xt, xt, pl.BlockSpec((1, 1, d), lambda i, m: (m[i], 0, 0)), vec, vec],
        out_specs=xt,
    )
    return pl.pallas_call(
        _ln2_body, grid_spec=grid_spec,
        out_shape=jax.ShapeDtypeStruct((n_tiles * TOK_TILE, d), F32),
        compiler_params=_cparams(("arbitrary",)), name="ln2",
    )(mod_idx, x1, ffn, g2, lg, lb)


def _mod_index(B, T, L, tile, n_tiles):
    start = np.arange(n_tiles) * tile
    return jnp.asarray(np.where(start < B * T, start // T, B), I32)


def _pad_w_in(w_in):
    d = w_in.shape[0]
    na, gla, conv, sgu, gates = (w_in[:, 0:768], w_in[:, 768:1568], w_in[:, 1568:2080],
                                 w_in[:, 2080:2592], w_in[:, 2592:6688])
    z224 = jnp.zeros((d, 1024 - 800), w_in.dtype)
    z256 = jnp.zeros((d, 1024 - 768), w_in.dtype)
    return jnp.concatenate([gates, conv, sgu, gla, z224, na, z256], axis=1).astype(BF16)


def kernel(x, c, ctx, c_ctx, ada_w, ada_b, w_in, na_rpb, gla_gate_up, gla_gate_b, gla_norm_g, conv_dw, conv_b,
           conv_ln_g, conv_ln_b, sgu_ln_g, sgu_ln_b, sgu_ws, sgu_bs, w_branch, w_out, ln1_g, ln1_b, peer_wq,
           peer_keys, peer_u, peer_v, ln2_g, ln2_b):
    B, T, D = x.shape
    L = ctx.shape[1]
    depth = ada_w.shape[0]
    n_lat, n_ctx = B * T, B * L
    ntok = n_lat + n_ctx
    kw, vw = GLA_HEADS * GLA_DK, GLA_HEADS * GLA_DV

    xa = jnp.concatenate([x.reshape(n_lat, D), ctx.reshape(n_ctx, D)], axis=0)
    cos_l, sin_l = _rope_tables(T)
    cos_c, sin_c = jnp.ones((L, kw), F32), jnp.zeros((L, kw), F32)
    s_zero = jnp.zeros((B, kw, vw), F32)
    n_mod = -(-(B + 1) // SUBLANES) * SUBLANES
    cpad = jnp.concatenate([c, c_ctx[None, :], jnp.zeros((n_mod - B - 1, D), F32)], axis=0)

    for l in range(depth):
        need_ctx = l < depth - 1
        n_act = ntok if need_ctx else n_lat
        mod = _ada(cpad, ada_w[l], ada_b[l][None, :])
        sh1, sc1, g1, sh2, sc2, g2 = [m[:, None, :] for m in jnp.split(mod, 6, axis=-1)]

        z = _modmm(xa, sc1, sh1, _pad_w_in(w_in[l]), _mod_index(B, T, L, TOK_TILE, ntok // TOK_TILE),
                   ntok // TOK_TILE, 1024)

        y_na = _na_attention(z, _na_bias_table(na_rpb[l]), B, T, L)

        gups = []
        for d in range(2):
            gu = jnp.zeros((LANES, kw), F32).at[d * GLA_GATE_RANK:(d + 1) * GLA_GATE_RANK].set(gla_gate_up[l, d])
            gups.append(gu.astype(BF16))
        gbs = [gla_gate_b[l, d][None, :] for d in range(2)]
        ctx_blocks = n_lat // L
        oc_f, sc_f = _gla_scan(z, cos_c, sin_c, gups[0], gbs[0], s_zero, B=B, seq=L, row0_blocks=ctx_blocks,
                               tile=L, reverse=False)
        oc_b, sc_b = _gla_scan(z, cos_c, sin_c, gups[1], gbs[1], s_zero, B=B, seq=L, row0_blocks=ctx_blocks,
                               tile=L, reverse=True)
        o_f, _ = _gla_scan(z, cos_l, sin_l, gups[0], gbs[0], sc_f, B=B, seq=T, row0_blocks=0, tile=TOK_TILE,
                           reverse=False)
        o_b, _ = _gla_scan(z, cos_l, sin_l, gups[1], gbs[1], sc_b, B=B, seq=T, row0_blocks=0, tile=TOK_TILE,
                           reverse=True)

        cv_args = (conv_dw[l], conv_b[l][None, :], conv_ln_g[l][None, :], conv_ln_b[l][None, :])
        y_cv = _conformer_conv(z, *cv_args, B=B, seq=T, row0_blocks=0)
        bs_exp = jnp.repeat(sgu_bs[l].T, BRANCH_W // SGU_GROUPS, axis=1)
        y_sg = _spatial_gating(z, sgu_ln_g[l][None, :], sgu_ln_b[l][None, :], sgu_ws[l].astype(BF16), bs_exp,
                               n_act // TOK_TILE)

        if need_ctx:
            y_na = jnp.concatenate([y_na, _ctx_attention(z, B, T, L)], axis=0)
            o_f = jnp.concatenate([o_f, oc_f], axis=0)
            o_b = jnp.concatenate([o_b, oc_b], axis=0)
            y_cv = jnp.concatenate([y_cv, _conformer_conv(z, *cv_args, B=B, seq=L, row0_blocks=ctx_blocks)], axis=0)

        x1, h2 = _merge(z, y_na, o_f, o_b, y_cv, y_sg, xa, g1, sc2, sh2, gla_norm_g[l].reshape(1, vw),
                        w_branch[l].astype(BF16), w_out[l].astype(BF16), ln1_g[l][None, :], ln1_b[l][None, :],
                        _mod_index(B, T, L, MERGE_TILE, n_act // MERGE_TILE), n_act // MERGE_TILE)

        q = _mm(h2, peer_wq[l].astype(BF16), n_act // TOK_TILE, 1024)
        idx_t, g_t = _peer_topk(q, peer_keys[l].astype(BF16), n_act // TOPK_TILE)
        idx = idx_t.reshape(N_SEL, n_act).T
        gate = g_t.reshape(N_SEL, n_act).T
        w = _peer_u(idx, h2.reshape(n_act, SUBLANES, LANES), gate, _pack_table(peer_u[l]), n_act // GATHER_TILE)
        ffn = _peer_v(idx, w, _pack_table(peer_v[l]), n_act // GATHER_TILE).reshape(n_act, D)

        xa = _ln2(x1, ffn, g2, ln2_g[l][None, :], ln2_b[l][None, :],
                  _mod_index(B, T, L, TOK_TILE, n_act // TOK_TILE), n_act // TOK_TILE)

    return xa[:n_lat].reshape(B, T, D)
```

```python
import functools

import numpy as np
import jax
import jax.numpy as jnp
from jax import lax
from jax.experimental import pallas as pl
from jax.experimental.pallas import tpu as pltpu

F32 = jnp.float32
BF16 = jnp.bfloat16
I32 = jnp.int32

D_MODEL = 1024
GRID_W = 64
BRANCH_W = 256
NA_HEADS = 4
NA_HEAD_DIM = 64
NA_WIN_H = 8
NA_WIN_W = 16
GLA_HEADS = 4
GLA_DV = 64
GLA_DK = 32
GLA_GATE_RANK = 16
GLA_TAU = 16.0
GLA_CHUNK = 64
ROPE_BASE = 100.0
CONV_WIDTH = 31
SGU_GROUPS = 4
SGU_CHUNK = 128
PEER_HEADS = 8
PEER_NKEYS = 128
PEER_DQ = 256
PEER_TOPK = 16
DEPTH = 2
ALPHA = (2 * DEPTH) ** 0.25
NEG_INF = -1e30
LN_EPS = 1e-6

SUBLANES = 8
LANES = 128

Z_GATES = 0
Z_CONV = 4096
Z_SGU = 4608
Z_GLA = 5120
Z_NA = 6144
Z_COLS = 7168

TOK_TILE = 512
GATHER_TILE = 128

_NT = (((1,), (1,)), ((), ()))
_TN = (((0,), (0,)), ((), ()))


def _cparams(sem, vmem_mb=48):
    return pltpu.CompilerParams(dimension_semantics=sem, vmem_limit_bytes=vmem_mb * 1024 * 1024)


def _split3(x):
    hi = x.astype(BF16)
    r1 = x - hi.astype(F32)
    mid = r1.astype(BF16)
    lo = (r1 - mid.astype(F32)).astype(BF16)
    return hi, mid, lo


def _dot_exact(a_bf16, x, dims=None, lhs_is_x=False):
    out = None
    for p in _split3(x):
        if dims is None:
            t = jnp.dot(p, a_bf16, preferred_element_type=F32) if lhs_is_x else jnp.dot(a_bf16, p, preferred_element_type=F32)
        else:
            t = lax.dot_general(p, a_bf16, dims, preferred_element_type=F32) if lhs_is_x else lax.dot_general(a_bf16, p, dims, preferred_element_type=F32)
        out = t if out is None else out + t
    return out


def _layer_norm(x, g, b):
    mu = jnp.mean(x, axis=-1, keepdims=True)
    xc = x - mu
    var = jnp.mean(xc * xc, axis=-1, keepdims=True)
    return xc * lax.rsqrt(var + LN_EPS) * g + b


ADA_TILE = 512


def _ada_body(c_ref, w_ref, b_ref, o_ref):
    c = c_ref[...]
    cs = c * jax.nn.sigmoid(c)
    o_ref[...] = jnp.dot(cs, w_ref[...], preferred_element_type=F32, precision=lax.Precision.HIGHEST) + b_ref[...]


def _ada(cpad, w, b):
    rows, d = cpad.shape
    n = w.shape[1]
    return pl.pallas_call(
        _ada_body, grid=(n // ADA_TILE,),
        in_specs=[pl.BlockSpec((rows, d), lambda j: (0, 0)),
                  pl.BlockSpec((d, ADA_TILE), lambda j: (0, j)),
                  pl.BlockSpec((1, ADA_TILE), lambda j: (0, j))],
        out_specs=pl.BlockSpec((rows, ADA_TILE), lambda j: (0, j)),
        out_shape=jax.ShapeDtypeStruct((rows, n), F32),
        compiler_params=_cparams(("arbitrary",)), name="ada",
    )(cpad, w, b)


def _modmm_body(mod_ref, x_ref, sc_ref, sh_ref, w_ref, o_ref):
    h = x_ref[...] * (1.0 + sc_ref[0]) + sh_ref[0]
    o_ref[...] = jnp.dot(h.astype(BF16), w_ref[...], preferred_element_type=F32)


def _mm_body(x_ref, w_ref, o_ref):
    o_ref[...] = jnp.dot(x_ref[...].astype(BF16), w_ref[...], preferred_element_type=F32)


def _modmm(x, sc, sh, w, mod_idx, n_tiles, tn):
    d = x.shape[1]
    n = w.shape[1]
    grid_spec = pltpu.PrefetchScalarGridSpec(
        num_scalar_prefetch=1,
        grid=(n // tn, n_tiles),
        in_specs=[
            pl.BlockSpec((TOK_TILE, d), lambda j, i, m: (i, 0)),
            pl.BlockSpec((1, 1, d), lambda j, i, m: (m[i], 0, 0)),
            pl.BlockSpec((1, 1, d), lambda j, i, m: (m[i], 0, 0)),
            pl.BlockSpec((d, tn), lambda j, i, m: (0, j)),
        ],
        out_specs=pl.BlockSpec((TOK_TILE, tn), lambda j, i, m: (i, j)),
    )
    return pl.pallas_call(
        _modmm_body, grid_spec=grid_spec,
        out_shape=jax.ShapeDtypeStruct((n_tiles * TOK_TILE, n), F32),
        compiler_params=_cparams(("arbitrary", "arbitrary")), name="modmm",
    )(mod_idx, x, sc, sh, w)


def _mm(x, w, n_tiles, tn):
    d = x.shape[1]
    n = w.shape[1]
    return pl.pallas_call(
        _mm_body, grid=(n // tn, n_tiles),
        in_specs=[pl.BlockSpec((TOK_TILE, d), lambda j, i: (i, 0)),
                  pl.BlockSpec((d, tn), lambda j, i: (0, j))],
        out_specs=pl.BlockSpec((TOK_TILE, tn), lambda j, i: (i, j)),
        out_shape=jax.ShapeDtypeStruct((n_tiles * TOK_TILE, n), F32),
        compiler_params=_cparams(("arbitrary", "arbitrary")), name="peer_q",
    )(x, w)


def _head_masks(width, per_head, heads):
    lane = lax.broadcasted_iota(I32, (1, width), 1)
    return [(lane >= h * per_head) & (lane < (h + 1) * per_head) for h in range(heads)]


def _na_body(q_ref, k_ref, v_ref, kc_ref, vc_ref, bias_ref, o_ref, *, rows):
    r = pl.program_id(1)
    r0 = jnp.clip(r - NA_WIN_H // 2, 0, rows - NA_WIN_H)
    delta = r - r0
    start = pl.multiple_of(r0 * GRID_W, GRID_W)
    nwin = NA_WIN_H * GRID_W
    kwin = k_ref[pl.ds(start, nwin), :].astype(BF16)
    vwin = v_ref[pl.ds(start, nwin), :].astype(BF16)
    kc = kc_ref[...].astype(BF16)
    vc = vc_ref[...].astype(BF16)
    q = q_ref[...] * (NA_HEAD_DIM ** -0.5)
    out = jnp.zeros(q.shape, F32)
    for h, hm in enumerate(_head_masks(BRANCH_W, NA_HEAD_DIM, NA_HEADS)):
        qh = jnp.where(hm, q, 0.0).astype(BF16)
        s = lax.dot_general(qh, kwin, _NT, preferred_element_type=F32) + bias_ref[h, delta]
        sc = lax.dot_general(qh, kc, _NT, preferred_element_type=F32)
        m = jnp.maximum(jnp.max(s, axis=1, keepdims=True), jnp.max(sc, axis=1, keepdims=True))
        e = jnp.exp(s - m)
        ec = jnp.exp(sc - m)
        den = jnp.sum(e, axis=1, keepdims=True) + jnp.sum(ec, axis=1, keepdims=True)
        oh = (jnp.dot(e.astype(BF16), vwin, preferred_element_type=F32)
              + jnp.dot(ec.astype(BF16), vc, preferred_element_type=F32))
        out = out + jnp.where(hm, oh / den, 0.0)
    o_ref[...] = out


def _na_bias_table(rpb):
    colv = np.arange(GRID_W)
    c0 = np.clip(colv - NA_WIN_W // 2, 0, GRID_W - NA_WIN_W)
    in_win = (colv[None, :] >= c0[:, None]) & (colv[None, :] < c0[:, None] + NA_WIN_W)
    edge = GRID_W - NA_WIN_W
    padded = jnp.pad(rpb, ((0, 0), (0, 0), (edge, edge)), mode="edge")
    cols = jnp.stack([padded[:, :, GRID_W - 1 - q:2 * GRID_W - 1 - q] for q in range(GRID_W)], axis=2)
    cols = jnp.where(jnp.asarray(in_win)[None, None], cols, NEG_INF)
    b = jnp.stack([cols[:, NA_WIN_H - 1 - d:2 * NA_WIN_H - 1 - d] for d in range(NA_WIN_H)], axis=1)
    b = b.transpose(0, 1, 3, 2, 4)
    return b.reshape(NA_HEADS, NA_WIN_H, GRID_W, NA_WIN_H * GRID_W).astype(F32)


def _na_attention(z, bias, B, T, L):
    rows = T // GRID_W
    cq, ck, cv = Z_NA // BRANCH_W, Z_NA // BRANCH_W + 1, Z_NA // BRANCH_W + 2
    ctx0 = (B * T) // L
    return pl.pallas_call(
        functools.partial(_na_body, rows=rows), grid=(B, rows),
        in_specs=[
            pl.BlockSpec((GRID_W, BRANCH_W), lambda b, r: (b * rows + r, cq)),
            pl.BlockSpec((T, BRANCH_W), lambda b, r: (b, ck)),
            pl.BlockSpec((T, BRANCH_W), lambda b, r: (b, cv)),
            pl.BlockSpec((L, BRANCH_W), lambda b, r: (ctx0 + b, ck)),
            pl.BlockSpec((L, BRANCH_W), lambda b, r: (ctx0 + b, cv)),
            pl.BlockSpec(bias.shape, lambda b, r: (0, 0, 0, 0)),
        ],
        out_specs=pl.BlockSpec((GRID_W, BRANCH_W), lambda b, r: (b * rows + r, 0)),
        out_shape=jax.ShapeDtypeStruct((B * T, BRANCH_W), F32),
        compiler_params=_cparams(("arbitrary", "arbitrary")), name="na_attn",
    )(z, z, z, z, z, bias)


def _ctx_attn_body(q_ref, k_ref, v_ref, o_ref):
    k = k_ref[...].astype(BF16)
    v = v_ref[...].astype(BF16)
    q = q_ref[...] * (NA_HEAD_DIM ** -0.5)
    out = jnp.zeros(q.shape, F32)
    for hm in _head_masks(BRANCH_W, NA_HEAD_DIM, NA_HEADS):
        qh = jnp.where(hm, q, 0.0).astype(BF16)
        s = lax.dot_general(qh, k, _NT, preferred_element_type=F32)
        m = jnp.max(s, axis=1, keepdims=True)
        e = jnp.exp(s - m)
        den = jnp.sum(e, axis=1, keepdims=True)
        oh = jnp.dot(e.astype(BF16), v, preferred_element_type=F32)
        out = out + jnp.where(hm, oh / den, 0.0)
    o_ref[...] = out


def _ctx_attention(z, B, T, L):
    cq = Z_NA // BRANCH_W
    ctx0 = (B * T) // L
    return pl.pallas_call(
        _ctx_attn_body, grid=(B,),
        in_specs=[pl.BlockSpec((L, BRANCH_W), lambda b, c=c: (ctx0 + b, cq + c)) for c in range(3)],
        out_specs=pl.BlockSpec((L, BRANCH_W), lambda b: (b, 0)),
        out_shape=jax.ShapeDtypeStruct((B * L, BRANCH_W), F32),
        compiler_params=_cparams(("arbitrary",)), name="ctx_attn",
    )(z, z, z)


def _gla_body(z_ref, cos_ref, sin_ref, gup_ref, gb_ref, s0_ref, o_ref, sfin_ref, state_ref, *, reverse, n_steps):
    g = pl.program_id(1)

    @pl.when(g == 0)
    def _():
        state_ref[...] = s0_ref[0]

    kw = GLA_HEADS * GLA_DK
    vw = GLA_HEADS * GLA_DV
    q = z_ref[:, 0:kw] * (GLA_DK ** -0.5)
    k = z_ref[:, kw:2 * kw]
    v = z_ref[:, 2 * kw:2 * kw + vw]
    lo = z_ref[:, 2 * kw + 2 * vw:2 * kw + 2 * vw + LANES]
    cos = cos_ref[...]
    sin = sin_ref[...]
    lane = lax.broadcasted_iota(I32, (1, kw), 1)
    first = (lane % (GLA_DK // 2)) < (GLA_DK // 4)

    def rope(x):
        partner = jnp.where(first, pltpu.roll(x, kw - GLA_DK // 4, 1), pltpu.roll(x, GLA_DK // 4, 1))
        return x * cos + partner * sin

    q = rope(q)
    k = rope(k)
    logits = jnp.dot(lo.astype(BF16), gup_ref[...], preferred_element_type=F32) + gb_ref[...]
    la = (jnp.minimum(logits, 0.0) - jnp.log1p(jnp.exp(-jnp.abs(logits)))) / GLA_TAU

    C = GLA_CHUNK
    ri = lax.broadcasted_iota(I32, (C, C), 0)
    ci = lax.broadcasted_iota(I32, (C, C), 1)
    tri = (ri <= ci) if reverse else (ri >= ci)
    cum = jnp.where(tri, 1.0, 0.0).astype(BF16)
    tri4 = jnp.concatenate([tri] * GLA_HEADS, axis=0)
    hm_k = _head_masks(kw, GLA_DK, GLA_HEADS)
    hm_v = _head_masks(vw, GLA_DV, GLA_HEADS)
    srow = lax.broadcasted_iota(I32, (kw, vw), 0) // GLA_DK
    scol = lax.broadcasted_iota(I32, (kw, vw), 1) // GLA_DV
    blockmask = srow == scol
    ones_cv = jnp.ones((C, vw), BF16)

    S = state_ref[...]
    n_chunks = z_ref.shape[0] // C
    order = range(n_chunks - 1, -1, -1) if reverse else range(n_chunks)
    for c in order:
        sl = slice(c * C, (c + 1) * C)
        la_c = la[sl]
        b = _dot_exact(cum, la_c)
        bl = b[0:1] if reverse else b[C - 1:C]
        qs = q[sl] * jnp.exp(b)
        ks = k[sl] * jnp.exp(-b)
        ke = k[sl] * jnp.exp(bl - b)
        vb = v[sl].astype(BF16)
        qs_b = qs.astype(BF16)
        qstack = jnp.concatenate([jnp.where(hm, qs, 0.0) for hm in hm_k], axis=0).astype(BF16)
        a = lax.dot_general(qstack, ks.astype(BF16), _NT, preferred_element_type=F32)
        a = jnp.where(tri4, a, 0.0)
        o_stack = jnp.dot(a.astype(BF16), vb, preferred_element_type=F32)
        o_c = jnp.dot(qs_b, S.astype(BF16), preferred_element_type=F32)
        for h, hm in enumerate(hm_v):
            o_c = o_c + jnp.where(hm, o_stack[h * C:(h + 1) * C], 0.0)
        o_ref[sl, :] = o_c
        u = lax.dot_general(ke.astype(BF16), vb, _TN, preferred_element_type=F32)
        dcol = _dot_exact(ones_cv, la_c, dims=_TN, lhs_is_x=True)
        S = jnp.exp(dcol) * S + jnp.where(blockmask, u, 0.0)
    state_ref[...] = S

    @pl.when(g == n_steps - 1)
    def _():
        sfin_ref[0] = S


def _gla_scan(z, cos, sin, gup, gb, s0, *, B, seq, row0_blocks, tile, reverse):
    n_steps = seq // tile
    kw = GLA_HEADS * GLA_DK
    vw = GLA_HEADS * GLA_DV

    def step(g):
        return n_steps - 1 - g if reverse else g

    return pl.pallas_call(
        functools.partial(_gla_body, reverse=reverse, n_steps=n_steps), grid=(B, n_steps),
        in_specs=[
            pl.BlockSpec((tile, 1024), lambda b, g: (row0_blocks + b * n_steps + step(g), Z_GLA // 1024)),
            pl.BlockSpec((tile, kw), lambda b, g: (step(g), 0)),
            pl.BlockSpec((tile, kw), lambda b, g: (step(g), 0)),
            pl.BlockSpec((LANES, kw), lambda b, g: (0, 0)),
            pl.BlockSpec((1, kw), lambda b, g: (0, 0)),
            pl.BlockSpec((1, kw, vw), lambda b, g: (b, 0, 0)),
        ],
        out_specs=[
            pl.BlockSpec((tile, vw), lambda b, g: (b * n_steps + step(g), 0)),
            pl.BlockSpec((1, kw, vw), lambda b, g: (b, 0, 0)),
        ],
        out_shape=[jax.ShapeDtypeStruct((B * seq, vw), F32), jax.ShapeDtypeStruct((B, kw, vw), F32)],
        scratch_shapes=[pltpu.VMEM((kw, vw), F32)],
        compiler_params=_cparams(("arbitrary", "arbitrary")), name="gla_rev" if reverse else "gla_fwd",
    )(z, cos, sin, gup, gb, s0)


def _rope_tables(T):
    t = np.arange(T)
    row, col = t // GRID_W, t % GRID_W
    nf = GLA_DK // 4
    inv = 1.0 / (ROPE_BASE ** (jnp.arange(nf, dtype=F32) / nf))
    j = np.arange(GLA_HEADS * GLA_DK)
    d = j % GLA_DK
    use_col = (d // (GLA_DK // 2)) == 1
    e = d % (GLA_DK // 2)
    fi = e % nf
    pos = jnp.where(jnp.asarray(use_col)[None, :], jnp.asarray(col, F32)[:, None], jnp.asarray(row, F32)[:, None])
    ang = pos * inv[fi][None, :]
    sign = jnp.asarray(np.where(e < nf, -1.0, 1.0), F32)[None, :]
    return jnp.cos(ang), jnp.sin(ang) * sign


CONV_PAD = 16
CONV_ROWS = 128


def _conv_body(z_ref, dw_ref, b_ref, g_ref, be_ref, o_ref, ypad_ref, *, seq):
    zeros = jnp.zeros((CONV_PAD, BRANCH_W), F32)
    ypad_ref[0:CONV_PAD, :] = zeros
    ypad_ref[seq + CONV_PAD:seq + 2 * CONV_PAD, :] = zeros

    def glu(i, carry):
        base = pl.multiple_of(i * CONV_ROWS, CONV_ROWS)
        a = z_ref[pl.ds(base, CONV_ROWS), 0:BRANCH_W]
        gate = z_ref[pl.ds(base, CONV_ROWS), BRANCH_W:2 * BRANCH_W]
        ypad_ref[pl.ds(base + CONV_PAD, CONV_ROWS), :] = a * jax.nn.sigmoid(gate)
        return carry

    lax.fori_loop(0, seq // CONV_ROWS, glu, 0)

    def tile(i, carry):
        base = pl.multiple_of(i * CONV_ROWS, CONV_ROWS)
        acc = jnp.zeros((CONV_ROWS, BRANCH_W), F32)
        win = ypad_ref[pl.ds(base, CONV_ROWS + 2 * CONV_PAD), :]
        for j in range(CONV_WIDTH):
            off = CONV_PAD - CONV_WIDTH // 2 + j
            acc = acc + win[off:off + CONV_ROWS] * dw_ref[j:j + 1, :]
        y = _layer_norm(acc + b_ref[...], g_ref[...], be_ref[...])
        o_ref[pl.ds(base, CONV_ROWS), :] = y * jax.nn.sigmoid(y)
        return carry

    lax.fori_loop(0, seq // CONV_ROWS, tile, 0)


def _conformer_conv(z, dw, b, g, be, *, B, seq, row0_blocks):
    vec = pl.BlockSpec((1, BRANCH_W), lambda i: (0, 0))
    return pl.pallas_call(
        functools.partial(_conv_body, seq=seq), grid=(B,),
        in_specs=[pl.BlockSpec((seq, 2 * BRANCH_W), lambda i: (row0_blocks + i, Z_CONV // (2 * BRANCH_W))),
                  pl.BlockSpec((CONV_WIDTH, BRANCH_W), lambda i: (0, 0)), vec, vec, vec],
        out_specs=pl.BlockSpec((seq, BRANCH_W), lambda i: (i, 0)),
        out_shape=jax.ShapeDtypeStruct((B * seq, BRANCH_W), F32),
        scratch_shapes=[pltpu.VMEM((seq + 2 * CONV_PAD, BRANCH_W), F32)],
        compiler_params=_cparams(("arbitrary",)), name="conformer_conv",
    )(z, dw, b, g, be)


def _sgu_body(z_ref, g_ref, b_ref, ws_ref, bs_ref, o_ref):
    zz = jax.nn.gelu(z_ref[...])
    u = zz[:, 0:BRANCH_W]
    v = _layer_norm(zz[:, BRANCH_W:2 * BRANCH_W], g_ref[...], b_ref[...])
    gms = _head_masks(BRANCH_W, BRANCH_W // SGU_GROUPS, SGU_GROUPS)
    for c in range(z_ref.shape[0] // SGU_CHUNK):
        sl = slice(c * SGU_CHUNK, (c + 1) * SGU_CHUNK)
        vb = v[sl].astype(BF16)
        s = bs_ref[...]
        for gi, gm in enumerate(gms):
            s = s + jnp.where(gm, jnp.dot(ws_ref[gi], vb, preferred_element_type=F32), 0.0)
        o_ref[sl, :] = u[sl] * s


def _spatial_gating(z, g, b, ws, bs_exp, n_tiles):
    vec = pl.BlockSpec((1, BRANCH_W), lambda i: (0, 0))
    return pl.pallas_call(
        _sgu_body, grid=(n_tiles,),
        in_specs=[pl.BlockSpec((TOK_TILE, 2 * BRANCH_W), lambda i: (i, Z_SGU // (2 * BRANCH_W))), vec, vec,
                  pl.BlockSpec(ws.shape, lambda i: (0, 0, 0)),
                  pl.BlockSpec((SGU_CHUNK, BRANCH_W), lambda i: (0, 0))],
        out_specs=pl.BlockSpec((TOK_TILE, BRANCH_W), lambda i: (i, 0)),
        out_shape=jax.ShapeDtypeStruct((n_tiles * TOK_TILE, BRANCH_W), F32),
        compiler_params=_cparams(("arbitrary",)), name="sgu",
    )(z, g, b, ws, bs_exp)


MERGE_TILE = 256


def _merge_body(mod_ref, gates_ref, yna_ref, of_ref, ob_ref, r_ref, ycv_ref, ysg_ref, x_ref, g1_ref, sc2_ref, sh2_ref,
                ng_ref, wb_ref, wo_ref, lg_ref, lb_ref, x1_ref, h2_ref):
    o = of_ref[...] + ob_ref[...]
    vw = GLA_HEADS * GLA_DV
    hr = lax.broadcasted_iota(I32, (vw, vw), 0) // GLA_DV
    hc = lax.broadcasted_iota(I32, (vw, vw), 1) // GLA_DV
    same_head = jnp.where(hr == hc, 1.0, 0.0).astype(BF16)
    ms = _dot_exact(same_head, o * o, lhs_is_x=True) * (1.0 / GLA_DV)
    r = r_ref[...]
    y_gla = o * lax.rsqrt(ms + LN_EPS) * ng_ref[...] * (r * jax.nn.sigmoid(r))
    ys = (yna_ref[...], y_gla, ycv_ref[...], ysg_ref[...])
    merged = None
    for i in range(4):
        gate = jax.nn.sigmoid(gates_ref[:, i * D_MODEL:(i + 1) * D_MODEL])
        term = gate * jnp.dot(ys[i].astype(BF16), wb_ref[i], preferred_element_type=F32)
        merged = term if merged is None else merged + term
    y = jnp.dot(merged.astype(BF16), wo_ref[...], preferred_element_type=F32)
    x1 = _layer_norm(ALPHA * x_ref[...] + g1_ref[0] * y, lg_ref[...], lb_ref[...])
    x1_ref[...] = x1
    h2_ref[...] = x1 * (1.0 + sc2_ref[0]) + sh2_ref[0]


def _merge(z, yna, of, ob, ycv, ysg, x, g1, sc2, sh2, ng, wb, wo, lg, lb, mod_idx, n_tiles):
    d = D_MODEL
    br = pl.BlockSpec((MERGE_TILE, BRANCH_W), lambda i, m: (i, 0))
    modv = pl.BlockSpec((1, 1, d), lambda i, m: (m[i], 0, 0))
    vec = pl.BlockSpec((1, d), lambda i, m: (0, 0))
    xt = pl.BlockSpec((MERGE_TILE, d), lambda i, m: (i, 0))
    grid_spec = pltpu.PrefetchScalarGridSpec(
        num_scalar_prefetch=1, grid=(n_tiles,),
        in_specs=[
            pl.BlockSpec((MERGE_TILE, 4 * d), lambda i, m: (i, 0)),
            br, br, br,
            pl.BlockSpec((MERGE_TILE, BRANCH_W), lambda i, m: (i, (Z_GLA + 512) // BRANCH_W)),
            br, br, xt, modv, modv, modv,
            pl.BlockSpec((1, BRANCH_W), lambda i, m: (0, 0)),
            pl.BlockSpec(wb.shape, lambda i, m: (0, 0, 0)),
            pl.BlockSpec(wo.shape, lambda i, m: (0, 0)),
            vec, vec,
        ],
        out_specs=[xt, xt],
    )
    return pl.pallas_call(
        _merge_body, grid_spec=grid_spec,
        out_shape=[jax.ShapeDtypeStruct((n_tiles * MERGE_TILE, d), F32)] * 2,
        compiler_params=_cparams(("arbitrary",)), name="merge",
    )(mod_idx, z, yna, of, ob, z, ycv, ysg, x, g1, sc2, sh2, ng, wb, wo, lg, lb)


TOPK_TILE = 256


def _top16(vals, payload=None):
    n_cand = vals.shape[0]
    pos_iota = lax.broadcasted_iota(I32, vals.shape, 0)
    out_v, out_p = [], []
    for _ in range(PEER_TOPK):
        m = jnp.max(vals, axis=0, keepdims=True)
        pos = jnp.min(jnp.where(vals == m, pos_iota, n_cand), axis=0, keepdims=True)
        hit = pos_iota == pos
        out_v.append(m)
        out_p.append(pos if payload is None else jnp.max(jnp.where(hit, payload, -1), axis=0, keepdims=True))
        vals = jnp.where(hit, -jnp.inf, vals)
    return jnp.concatenate(out_v, axis=0), jnp.concatenate(out_p, axis=0)


_PAIR_COUNTS = [PEER_TOPK // (i + 1) for i in range(PEER_TOPK)]


def _topk_body(q_ref, keys_ref, idx_ref, g_ref):
    half = PEER_DQ // 2
    n_tok = q_ref.shape[0]
    tops = []
    for s in range(2):
        qs = q_ref[:, s * half:(s + 1) * half].astype(BF16)
        sc = lax.dot_general(keys_ref[0, s], qs, _NT, preferred_element_type=F32)
        tops.append(_top16(sc))
    (a, ia), (b, ib) = tops
    cand = [a[i:i + 1] + b[0:n] for i, n in enumerate(_PAIR_COUNTS)]
    code = [ia[i:i + 1] * PEER_NKEYS + ib[0:n] for i, n in enumerate(_PAIR_COUNTS)]
    pad = -sum(_PAIR_COUNTS) % SUBLANES
    cand = jnp.concatenate(cand + [jnp.full((pad, n_tok), -jnp.inf, F32)], axis=0)
    code = jnp.concatenate(code + [jnp.zeros((pad, n_tok), I32)], axis=0)
    best, idx = _top16(cand, code)
    e = jnp.exp(best - jnp.max(best, axis=0, keepdims=True))
    g_ref[0] = e / jnp.sum(e, axis=0, keepdims=True)
    idx_ref[0] = idx


def _peer_topk(q, keys, n_tiles):
    ntok = n_tiles * TOPK_TILE
    out = pl.BlockSpec((1, PEER_TOPK, TOPK_TILE), lambda i, h: (h, 0, i))
    return pl.pallas_call(
        _topk_body, grid=(n_tiles, PEER_HEADS),
        in_specs=[pl.BlockSpec((TOPK_TILE, PEER_DQ), lambda i, h: (i, h)),
                  pl.BlockSpec((1, 2, PEER_NKEYS, PEER_DQ // 2), lambda i, h: (h, 0, 0, 0))],
        out_specs=[out, out],
        out_shape=[jax.ShapeDtypeStruct((PEER_HEADS, PEER_TOPK, ntok), I32),
                   jax.ShapeDtypeStruct((PEER_HEADS, PEER_TOPK, ntok), F32)],
        compiler_params=_cparams(("arbitrary", "arbitrary")), name="peer_topk",
    )(q, keys)


HALF_SUB = SUBLANES // 2
N_SEL = PEER_HEADS * PEER_TOPK
SEL_ROWS = SUBLANES * N_SEL
U_LOW_ROWS, U_HIGH_ROWS = (7, 5, 3, 1), (6, 4, 2, 0)
V_LOW_ROWS, V_HIGH_ROWS = (0, 2, 4, 6), (1, 3, 5, 7)


def _pack_table(tab, low_rows, high_rows):
    e = tab.shape[0]
    bits = lax.bitcast_convert_type(tab.astype(BF16), jnp.uint16).astype(jnp.uint32).reshape(e, SUBLANES, LANES)
    low = jnp.stack([bits[:, i] for i in low_rows], axis=1)
    high = jnp.stack([bits[:, i] for i in high_rows], axis=1)
    return lax.bitcast_convert_type(low | (high << 16), I32)


def _stage_rows(idx_ref, tab_ref, stage_ref, p):
    for k in range(N_SEL):
        stage_ref[k * HALF_SUB:(k + 1) * HALF_SUB, :] = tab_ref[idx_ref[p, k]]


def _hi_lo_rows(x):
    hi = x.astype(BF16).astype(F32)
    return jnp.concatenate([hi, x - hi], axis=0).astype(BF16)


def _two_stage_token_loop(n_tok, stage, compute):
    stage(0, 0)

    def pair(j, carry):
        p = 2 * j
        stage(p + 1, 1)
        compute(p, 0)
        stage(jnp.minimum(p + 2, n_tok - 1), 0)
        compute(p + 1, 1)
        return carry

    lax.fori_loop(0, n_tok // 2, pair, 0)


def _peer_u_body(idx_ref, h_ref, g_ref, tab_ref, w_ref, stage0_ref, stage1_ref, a_ref):
    stages = (stage0_ref, stage1_ref)

    def compute(p, slot):
        m = pltpu.bitcast(stages[slot][...], BF16)
        y = lax.dot_general(_hi_lo_rows(h_ref[p]), m, _NT, preferred_element_type=F32)
        y = y[0:SUBLANES] + y[SUBLANES:2 * SUBLANES]
        pieces = []
        for v in range(SEL_ROWS // LANES):
            z = pltpu.roll(y[:, v * LANES:(v + 1) * LANES], LANES - SUBLANES + 1, 1, stride=1, stride_axis=0)
            pieces.append(jnp.sum(z, axis=0, keepdims=True))
        a_ref[pl.ds(p, 1), :] = jnp.concatenate(pieces, axis=1)

    _two_stage_token_loop(h_ref.shape[0], lambda p, slot: _stage_rows(idx_ref, tab_ref, stages[slot], p), compute)
    w_ref[...] = g_ref[...] * jax.nn.gelu(a_ref[...])


def _peer_v_body(idx_ref, w_ref, tab_ref, o_ref, stage0_ref, stage1_ref):
    stages = (stage0_ref, stage1_ref)

    def compute(p, slot):
        m = pltpu.bitcast(stages[slot][...], BF16)
        wrow = w_ref[pl.ds(p, 1), :]
        pieces = []
        for v in range(SEL_ROWS // LANES):
            wb = jnp.broadcast_to(wrow[:, v * LANES:(v + 1) * LANES], (SUBLANES, LANES))
            pieces.append(pltpu.roll(wb, 0, 1, stride=1, stride_axis=0))
        out = jnp.dot(_hi_lo_rows(jnp.concatenate(pieces, axis=1)), m, preferred_element_type=F32)
        o_ref[p] = out[0:SUBLANES] + out[SUBLANES:2 * SUBLANES]

    _two_stage_token_loop(o_ref.shape[0], lambda p, slot: _stage_rows(idx_ref, tab_ref, stages[slot], p), compute)


def _table_spec(tab):
    return pl.BlockSpec(tab.shape, lambda i: (0, 0, 0), pipeline_mode=pl.Buffered(1))


def _peer_u(idx, h3, g_sparse, tab, n_tiles):
    tb = GATHER_TILE
    wide = pl.BlockSpec((tb, SEL_ROWS), lambda i: (i, 0))
    return pl.pallas_call(
        _peer_u_body, grid=(n_tiles,),
        in_specs=[pl.BlockSpec((tb, N_SEL), lambda i: (i, 0), memory_space=pltpu.SMEM),
                  pl.BlockSpec((tb, SUBLANES, LANES), lambda i: (i, 0, 0)), wide, _table_spec(tab)],
        out_specs=wide,
        out_shape=jax.ShapeDtypeStruct((n_tiles * tb, SEL_ROWS), F32),
        scratch_shapes=[pltpu.VMEM((HALF_SUB * N_SEL, LANES), I32)] * 2 + [pltpu.VMEM((tb, SEL_ROWS), F32)],
        compiler_params=_cparams(("arbitrary",), vmem_mb=56), name="peer_u",
    )(idx, h3, g_sparse, tab)


def _peer_v(idx, w_sparse, tab, n_tiles):
    tb = GATHER_TILE
    return pl.pallas_call(
        _peer_v_body, grid=(n_tiles,),
        in_specs=[pl.BlockSpec((tb, N_SEL), lambda i: (i, 0), memory_space=pltpu.SMEM),
                  pl.BlockSpec((tb, SEL_ROWS), lambda i: (i, 0)), _table_spec(tab)],
        out_specs=pl.BlockSpec((tb, SUBLANES, LANES), lambda i: (i, 0, 0)),
        out_shape=jax.ShapeDtypeStruct((n_tiles * tb, SUBLANES, LANES), F32),
        scratch_shapes=[pltpu.VMEM((HALF_SUB * N_SEL, LANES), I32)] * 2,
        compiler_params=_cparams(("arbitrary",), vmem_mb=56), name="peer_v",
    )(idx, w_sparse, tab)


def _ln2_body(mod_ref, x_ref, f_ref, g2_ref, lg_ref, lb_ref, o_ref):
    o_ref[...] = _layer_norm(ALPHA * x_ref[...] + g2_ref[0] * f_ref[...], lg_ref[...], lb_ref[...])


def _ln2(x1, ffn, g2, lg, lb, mod_idx, n_tiles):
    d = D_MODEL
    xt = pl.BlockSpec((TOK_TILE, d), lambda i, m: (i, 0))
    vec = pl.BlockSpec((1, d), lambda i, m: (0, 0))
    grid_spec = pltpu.PrefetchScalarGridSpec(
        num_scalar_prefetch=1, grid=(n_tiles,),
        in_specs=[xt, xt, pl.BlockSpec((1, 1, d), lambda i, m: (m[i], 0, 0)), vec, vec],
        out_specs=xt,
    )
    return pl.pallas_call(
        _ln2_body, grid_spec=grid_spec,
        out_shape=jax.ShapeDtypeStruct((n_tiles * TOK_TILE, d), F32),
        compiler_params=_cparams(("arbitrary",)), name="ln2",
    )(mod_idx, x1, ffn, g2, lg, lb)


def _mod_index(B, T, L, tile, n_tiles):
    start = np.arange(n_tiles) * tile
    return jnp.asarray(np.where(start < B * T, start // T, B), I32)


def _pad_w_in(w_in):
    d = w_in.shape[0]
    na, gla, conv, sgu, gates = (w_in[:, 0:768], w_in[:, 768:1568], w_in[:, 1568:2080],
                                 w_in[:, 2080:2592], w_in[:, 2592:6688])
    z224 = jnp.zeros((d, 1024 - 800), w_in.dtype)
    z256 = jnp.zeros((d, 1024 - 768), w_in.dtype)
    return jnp.concatenate([gates, conv, sgu, gla, z224, na, z256], axis=1).astype(BF16)


def kernel(x, c, ctx, c_ctx, ada_w, ada_b, w_in, na_rpb, gla_gate_up, gla_gate_b, gla_norm_g, conv_dw, conv_b,
           conv_ln_g, conv_ln_b, sgu_ln_g, sgu_ln_b, sgu_ws, sgu_bs, w_branch, w_out, ln1_g, ln1_b, peer_wq,
           peer_keys, peer_u, peer_v, ln2_g, ln2_b):
    B, T, D = x.shape
    L = ctx.shape[1]
    depth = ada_w.shape[0]
    n_lat, n_ctx = B * T, B * L
    ntok = n_lat + n_ctx
    kw, vw = GLA_HEADS * GLA_DK, GLA_HEADS * GLA_DV

    xa = jnp.concatenate([x.reshape(n_lat, D), ctx.reshape(n_ctx, D)], axis=0)
    cos_l, sin_l = _rope_tables(T)
    cos_c, sin_c = jnp.ones((L, kw), F32), jnp.zeros((L, kw), F32)
    s_zero = jnp.zeros((B, kw, vw), F32)
    n_mod = -(-(B + 1) // SUBLANES) * SUBLANES
    cpad = jnp.concatenate([c, c_ctx[None, :], jnp.zeros((n_mod - B - 1, D), F32)], axis=0)

    for l in range(depth):
        need_ctx = l < depth - 1
        n_act = ntok if need_ctx else n_lat
        mod = _ada(cpad, ada_w[l], ada_b[l][None, :])
        sh1, sc1, g1, sh2, sc2, g2 = [m[:, None, :] for m in jnp.split(mod, 6, axis=-1)]

        z = _modmm(xa, sc1, sh1, _pad_w_in(w_in[l]), _mod_index(B, T, L, TOK_TILE, ntok // TOK_TILE),
                   ntok // TOK_TILE, 1024)

        y_na = _na_attention(z, _na_bias_table(na_rpb[l]), B, T, L)

        gups = []
        for d in range(2):
            gu = jnp.zeros((LANES, kw), F32).at[d * GLA_GATE_RANK:(d + 1) * GLA_GATE_RANK].set(gla_gate_up[l, d])
            gups.append(gu.astype(BF16))
        gbs = [gla_gate_b[l, d][None, :] for d in range(2)]
        ctx_blocks = n_lat // L
        oc_f, sc_f = _gla_scan(z, cos_c, sin_c, gups[0], gbs[0], s_zero, B=B, seq=L, row0_blocks=ctx_blocks,
                               tile=L, reverse=False)
        oc_b, sc_b = _gla_scan(z, cos_c, sin_c, gups[1], gbs[1], s_zero, B=B, seq=L, row0_blocks=ctx_blocks,
                               tile=L, reverse=True)
        o_f, _ = _gla_scan(z, cos_l, sin_l, gups[0], gbs[0], sc_f, B=B, seq=T, row0_blocks=0, tile=TOK_TILE,
                           reverse=False)
        o_b, _ = _gla_scan(z, cos_l, sin_l, gups[1], gbs[1], sc_b, B=B, seq=T, row0_blocks=0, tile=TOK_TILE,
                           reverse=True)

        cv_args = (conv_dw[l], conv_b[l][None, :], conv_ln_g[l][None, :], conv_ln_b[l][None, :])
        y_cv = _conformer_conv(z, *cv_args, B=B, seq=T, row0_blocks=0)
        bs_exp = jnp.repeat(sgu_bs[l].T, BRANCH_W // SGU_GROUPS, axis=1)
        y_sg = _spatial_gating(z, sgu_ln_g[l][None, :], sgu_ln_b[l][None, :], sgu_ws[l].astype(BF16), bs_exp,
                               n_act // TOK_TILE)

        if need_ctx:
            y_na = jnp.concatenate([y_na, _ctx_attention(z, B, T, L)], axis=0)
            o_f = jnp.concatenate([o_f, oc_f], axis=0)
            o_b = jnp.concatenate([o_b, oc_b], axis=0)
            y_cv = jnp.concatenate([y_cv, _conformer_conv(z, *cv_args, B=B, seq=L, row0_blocks=ctx_blocks)], axis=0)

        x1, h2 = _merge(z, y_na, o_f, o_b, y_cv, y_sg, xa, g1, sc2, sh2, gla_norm_g[l].reshape(1, vw),
                        w_branch[l].astype(BF16), w_out[l].astype(BF16), ln1_g[l][None, :], ln1_b[l][None, :],
                        _mod_index(B, T, L, MERGE_TILE, n_act // MERGE_TILE), n_act // MERGE_TILE)

        q = _mm(h2, peer_wq[l].astype(BF16), n_act // TOK_TILE, 1024)
        idx_t, g_t = _peer_topk(q, peer_keys[l].astype(BF16), n_act // TOPK_TILE)
        idx = idx_t.reshape(N_SEL, n_act).T
        gate = g_t.reshape(N_SEL, n_act).T
        g_sparse = jnp.pad(gate[:, :, None], ((0, 0), (0, 0), (0, SUBLANES - 1))).reshape(n_act, SEL_ROWS)
        w_sparse = _peer_u(idx, h2.reshape(n_act, SUBLANES, LANES), g_sparse,
                           _pack_table(peer_u[l], U_LOW_ROWS, U_HIGH_ROWS), n_act // GATHER_TILE)
        ffn = _peer_v(idx, w_sparse, _pack_table(peer_v[l], V_LOW_ROWS, V_HIGH_ROWS),
                      n_act // GATHER_TILE).reshape(n_act, D)

        xa = _ln2(x1, ffn, g2, ln2_g[l][None, :], ln2_b[l][None, :],
                  _mod_index(B, T, L, TOK_TILE, n_act // TOK_TILE), n_act // TOK_TILE)

    return xa[:n_lat].reshape(B, T, D)
```

```python
import functools

import numpy as np
import jax
import jax.numpy as jnp
from jax import lax
from jax.experimental import pallas as pl
from jax.experimental.pallas import tpu as pltpu

F32 = jnp.float32
BF16 = jnp.bfloat16
I32 = jnp.int32

D_MODEL = 1024
GRID_W = 64
BRANCH_W = 256
NA_HEADS = 4
NA_HEAD_DIM = 64
NA_WIN_H = 8
NA_WIN_W = 16
GLA_HEADS = 4
GLA_DV = 64
GLA_DK = 32
GLA_GATE_RANK = 16
GLA_TAU = 16.0
GLA_CHUNK = 64
ROPE_BASE = 100.0
CONV_WIDTH = 31
SGU_GROUPS = 4
SGU_CHUNK = 128
PEER_HEADS = 8
PEER_NKEYS = 128
PEER_DQ = 256
PEER_TOPK = 16
DEPTH = 2
ALPHA = (2 * DEPTH) ** 0.25
NEG_INF = -1e30
LN_EPS = 1e-6

SUBLANES = 8
LANES = 128

Z_GATES = 0
Z_CONV = 4096
Z_SGU = 4608
Z_GLA = 5120
Z_NA = 6144
Z_COLS = 7168

TOK_TILE = 512
GATHER_TILE = 128

_NT = (((1,), (1,)), ((), ()))
_TN = (((0,), (0,)), ((), ()))


def _cparams(sem, vmem_mb=48):
    return pltpu.CompilerParams(dimension_semantics=sem, vmem_limit_bytes=vmem_mb * 1024 * 1024)


def _split3(x):
    hi = x.astype(BF16)
    r1 = x - hi.astype(F32)
    mid = r1.astype(BF16)
    lo = (r1 - mid.astype(F32)).astype(BF16)
    return hi, mid, lo


def _dot_exact(a_bf16, x, dims=None, lhs_is_x=False):
    out = None
    for p in _split3(x):
        if dims is None:
            t = jnp.dot(p, a_bf16, preferred_element_type=F32) if lhs_is_x else jnp.dot(a_bf16, p, preferred_element_type=F32)
        else:
            t = lax.dot_general(p, a_bf16, dims, preferred_element_type=F32) if lhs_is_x else lax.dot_general(a_bf16, p, dims, preferred_element_type=F32)
        out = t if out is None else out + t
    return out


def _layer_norm(x, g, b):
    mu = jnp.mean(x, axis=-1, keepdims=True)
    xc = x - mu
    var = jnp.mean(xc * xc, axis=-1, keepdims=True)
    return xc * lax.rsqrt(var + LN_EPS) * g + b


ADA_TILE = 512


def _ada_body(c_ref, w_ref, b_ref, o_ref):
    c = c_ref[...]
    cs = c * jax.nn.sigmoid(c)
    o_ref[...] = jnp.dot(cs, w_ref[...], preferred_element_type=F32, precision=lax.Precision.HIGHEST) + b_ref[...]


def _ada(cpad, w, b):
    rows, d = cpad.shape
    n = w.shape[1]
    return pl.pallas_call(
        _ada_body, grid=(n // ADA_TILE,),
        in_specs=[pl.BlockSpec((rows, d), lambda j: (0, 0)),
                  pl.BlockSpec((d, ADA_TILE), lambda j: (0, j)),
                  pl.BlockSpec((1, ADA_TILE), lambda j: (0, j))],
        out_specs=pl.BlockSpec((rows, ADA_TILE), lambda j: (0, j)),
        out_shape=jax.ShapeDtypeStruct((rows, n), F32),
        compiler_params=_cparams(("arbitrary",)), name="ada",
    )(cpad, w, b)


def _modmm_body(mod_ref, x_ref, sc_ref, sh_ref, w_ref, o_ref):
    h = x_ref[...] * (1.0 + sc_ref[0]) + sh_ref[0]
    o_ref[...] = jnp.dot(h.astype(BF16), w_ref[...], preferred_element_type=F32)


def _mm_body(x_ref, w_ref, o_ref):
    o_ref[...] = jnp.dot(x_ref[...].astype(BF16), w_ref[...], preferred_element_type=F32)


def _modmm(x, sc, sh, w, mod_idx, n_tiles, tn):
    d = x.shape[1]
    n = w.shape[1]
    grid_spec = pltpu.PrefetchScalarGridSpec(
        num_scalar_prefetch=1,
        grid=(n // tn, n_tiles),
        in_specs=[
            pl.BlockSpec((TOK_TILE, d), lambda j, i, m: (i, 0)),
            pl.BlockSpec((1, 1, d), lambda j, i, m: (m[i], 0, 0)),
            pl.BlockSpec((1, 1, d), lambda j, i, m: (m[i], 0, 0)),
            pl.BlockSpec((d, tn), lambda j, i, m: (0, j)),
        ],
        out_specs=pl.BlockSpec((TOK_TILE, tn), lambda j, i, m: (i, j)),
    )
    return pl.pallas_call(
        _modmm_body, grid_spec=grid_spec,
        out_shape=jax.ShapeDtypeStruct((n_tiles * TOK_TILE, n), F32),
        compiler_params=_cparams(("arbitrary", "arbitrary")), name="modmm",
    )(mod_idx, x, sc, sh, w)


def _mm(x, w, n_tiles, tn):
    d = x.shape[1]
    n = w.shape[1]
    return pl.pallas_call(
        _mm_body, grid=(n // tn, n_tiles),
        in_specs=[pl.BlockSpec((TOK_TILE, d), lambda j, i: (i, 0)),
                  pl.BlockSpec((d, tn), lambda j, i: (0, j))],
        out_specs=pl.BlockSpec((TOK_TILE, tn), lambda j, i: (i, j)),
        out_shape=jax.ShapeDtypeStruct((n_tiles * TOK_TILE, n), F32),
        compiler_params=_cparams(("arbitrary", "arbitrary")), name="peer_q",
    )(x, w)


def _head_masks(width, per_head, heads):
    lane = lax.broadcasted_iota(I32, (1, width), 1)
    return [(lane >= h * per_head) & (lane < (h + 1) * per_head) for h in range(heads)]


def _na_body(q_ref, k_ref, v_ref, kc_ref, vc_ref, bias_ref, o_ref, *, rows):
    r = pl.program_id(1)
    r0 = jnp.clip(r - NA_WIN_H // 2, 0, rows - NA_WIN_H)
    delta = r - r0
    start = pl.multiple_of(r0 * GRID_W, GRID_W)
    nwin = NA_WIN_H * GRID_W
    kwin = k_ref[pl.ds(start, nwin), :].astype(BF16)
    vwin = v_ref[pl.ds(start, nwin), :].astype(BF16)
    kc = kc_ref[...].astype(BF16)
    vc = vc_ref[...].astype(BF16)
    q = q_ref[...] * (NA_HEAD_DIM ** -0.5)
    out = jnp.zeros(q.shape, F32)
    for h, hm in enumerate(_head_masks(BRANCH_W, NA_HEAD_DIM, NA_HEADS)):
        qh = jnp.where(hm, q, 0.0).astype(BF16)
        s = lax.dot_general(qh, kwin, _NT, preferred_element_type=F32) + bias_ref[h, delta]
        sc = lax.dot_general(qh, kc, _NT, preferred_element_type=F32)
        m = jnp.maximum(jnp.max(s, axis=1, keepdims=True), jnp.max(sc, axis=1, keepdims=True))
        e = jnp.exp(s - m)
        ec = jnp.exp(sc - m)
        den = jnp.sum(e, axis=1, keepdims=True) + jnp.sum(ec, axis=1, keepdims=True)
        oh = (jnp.dot(e.astype(BF16), vwin, preferred_element_type=F32)
              + jnp.dot(ec.astype(BF16), vc, preferred_element_type=F32))
        out = out + jnp.where(hm, oh / den, 0.0)
    o_ref[...] = out


def _na_bias_table(rpb):
    colv = np.arange(GRID_W)
    c0 = np.clip(colv - NA_WIN_W // 2, 0, GRID_W - NA_WIN_W)
    in_win = (colv[None, :] >= c0[:, None]) & (colv[None, :] < c0[:, None] + NA_WIN_W)
    edge = GRID_W - NA_WIN_W
    padded = jnp.pad(rpb, ((0, 0), (0, 0), (edge, edge)), mode="edge")
    cols = jnp.stack([padded[:, :, GRID_W - 1 - q:2 * GRID_W - 1 - q] for q in range(GRID_W)], axis=2)
    cols = jnp.where(jnp.asarray(in_win)[None, None], cols, NEG_INF)
    b = jnp.stack([cols[:, NA_WIN_H - 1 - d:2 * NA_WIN_H - 1 - d] for d in range(NA_WIN_H)], axis=1)
    b = b.transpose(0, 1, 3, 2, 4)
    return b.reshape(NA_HEADS, NA_WIN_H, GRID_W, NA_WIN_H * GRID_W).astype(F32)


def _na_attention(z, bias, B, T, L):
    rows = T // GRID_W
    cq, ck, cv = Z_NA // BRANCH_W, Z_NA // BRANCH_W + 1, Z_NA // BRANCH_W + 2
    ctx0 = (B * T) // L
    return pl.pallas_call(
        functools.partial(_na_body, rows=rows), grid=(B, rows),
        in_specs=[
            pl.BlockSpec((GRID_W, BRANCH_W), lambda b, r: (b * rows + r, cq)),
            pl.BlockSpec((T, BRANCH_W), lambda b, r: (b, ck)),
            pl.BlockSpec((T, BRANCH_W), lambda b, r: (b, cv)),
            pl.BlockSpec((L, BRANCH_W), lambda b, r: (ctx0 + b, ck)),
            pl.BlockSpec((L, BRANCH_W), lambda b, r: (ctx0 + b, cv)),
            pl.BlockSpec(bias.shape, lambda b, r: (0, 0, 0, 0)),
        ],
        out_specs=pl.BlockSpec((GRID_W, BRANCH_W), lambda b, r: (b * rows + r, 0)),
        out_shape=jax.ShapeDtypeStruct((B * T, BRANCH_W), F32),
        compiler_params=_cparams(("arbitrary", "arbitrary")), name="na_attn",
    )(z, z, z, z, z, bias)


def _ctx_attn_body(q_ref, k_ref, v_ref, o_ref):
    k = k_ref[...].astype(BF16)
    v = v_ref[...].astype(BF16)
    q = q_ref[...] * (NA_HEAD_DIM ** -0.5)
    out = jnp.zeros(q.shape, F32)
    for hm in _head_masks(BRANCH_W, NA_HEAD_DIM, NA_HEADS):
        qh = jnp.where(hm, q, 0.0).astype(BF16)
        s = lax.dot_general(qh, k, _NT, preferred_element_type=F32)
        m = jnp.max(s, axis=1, keepdims=True)
        e = jnp.exp(s - m)
        den = jnp.sum(e, axis=1, keepdims=True)
        oh = jnp.dot(e.astype(BF16), v, preferred_element_type=F32)
        out = out + jnp.where(hm, oh / den, 0.0)
    o_ref[...] = out


def _ctx_attention(z, B, T, L):
    cq = Z_NA // BRANCH_W
    ctx0 = (B * T) // L
    return pl.pallas_call(
        _ctx_attn_body, grid=(B,),
        in_specs=[pl.BlockSpec((L, BRANCH_W), lambda b, c=c: (ctx0 + b, cq + c)) for c in range(3)],
        out_specs=pl.BlockSpec((L, BRANCH_W), lambda b: (b, 0)),
        out_shape=jax.ShapeDtypeStruct((B * L, BRANCH_W), F32),
        compiler_params=_cparams(("arbitrary",)), name="ctx_attn",
    )(z, z, z)


def _gla_body(z_ref, cos_ref, sin_ref, gup_ref, gb_ref, s0_ref, o_ref, sfin_ref, state_ref, *, reverse, n_steps):
    g = pl.program_id(1)

    @pl.when(g == 0)
    def _():
        state_ref[...] = s0_ref[0]

    kw = GLA_HEADS * GLA_DK
    vw = GLA_HEADS * GLA_DV
    q = z_ref[:, 0:kw] * (GLA_DK ** -0.5)
    k = z_ref[:, kw:2 * kw]
    v = z_ref[:, 2 * kw:2 * kw + vw]
    lo = z_ref[:, 2 * kw + 2 * vw:2 * kw + 2 * vw + LANES]
    cos = cos_ref[...]
    sin = sin_ref[...]
    lane = lax.broadcasted_iota(I32, (1, kw), 1)
    first = (lane % (GLA_DK // 2)) < (GLA_DK // 4)

    def rope(x):
        partner = jnp.where(first, pltpu.roll(x, kw - GLA_DK // 4, 1), pltpu.roll(x, GLA_DK // 4, 1))
        return x * cos + partner * sin

    q = rope(q)
    k = rope(k)
    logits = jnp.dot(lo.astype(BF16), gup_ref[...], preferred_element_type=F32) + gb_ref[...]
    la = (jnp.minimum(logits, 0.0) - jnp.log1p(jnp.exp(-jnp.abs(logits)))) / GLA_TAU

    C = GLA_CHUNK
    ri = lax.broadcasted_iota(I32, (C, C), 0)
    ci = lax.broadcasted_iota(I32, (C, C), 1)
    tri = (ri <= ci) if reverse else (ri >= ci)
    cum = jnp.where(tri, 1.0, 0.0).astype(BF16)
    tri4 = jnp.concatenate([tri] * GLA_HEADS, axis=0)
    hm_k = _head_masks(kw, GLA_DK, GLA_HEADS)
    hm_v = _head_masks(vw, GLA_DV, GLA_HEADS)
    srow = lax.broadcasted_iota(I32, (kw, vw), 0) // GLA_DK
    scol = lax.broadcasted_iota(I32, (kw, vw), 1) // GLA_DV
    blockmask = srow == scol
    ones_cv = jnp.ones((C, vw), BF16)

    S = state_ref[...]
    n_chunks = z_ref.shape[0] // C
    order = range(n_chunks - 1, -1, -1) if reverse else range(n_chunks)
    for c in order:
        sl = slice(c * C, (c + 1) * C)
        la_c = la[sl]
        b = _dot_exact(cum, la_c)
        bl = b[0:1] if reverse else b[C - 1:C]
        qs = q[sl] * jnp.exp(b)
        ks = k[sl] * jnp.exp(-b)
        ke = k[sl] * jnp.exp(bl - b)
        vb = v[sl].astype(BF16)
        qs_b = qs.astype(BF16)
        qstack = jnp.concatenate([jnp.where(hm, qs, 0.0) for hm in hm_k], axis=0).astype(BF16)
        a = lax.dot_general(qstack, ks.astype(BF16), _NT, preferred_element_type=F32)
        a = jnp.where(tri4, a, 0.0)
        o_stack = jnp.dot(a.astype(BF16), vb, preferred_element_type=F32)
        o_c = jnp.dot(qs_b, S.astype(BF16), preferred_element_type=F32)
        for h, hm in enumerate(hm_v):
            o_c = o_c + jnp.where(hm, o_stack[h * C:(h + 1) * C], 0.0)
        o_ref[sl, :] = o_c
        u = lax.dot_general(ke.astype(BF16), vb, _TN, preferred_element_type=F32)
        dcol = _dot_exact(ones_cv, la_c, dims=_TN, lhs_is_x=True)
        S = jnp.exp(dcol) * S + jnp.where(blockmask, u, 0.0)
    state_ref[...] = S

    @pl.when(g == n_steps - 1)
    def _():
        sfin_ref[0] = S


def _gla_scan(z, cos, sin, gup, gb, s0, *, B, seq, row0_blocks, tile, reverse):
    n_steps = seq // tile
    kw = GLA_HEADS * GLA_DK
    vw = GLA_HEADS * GLA_DV

    def step(g):
        return n_steps - 1 - g if reverse else g

    return pl.pallas_call(
        functools.partial(_gla_body, reverse=reverse, n_steps=n_steps), grid=(B, n_steps),
        in_specs=[
            pl.BlockSpec((tile, 1024), lambda b, g: (row0_blocks + b * n_steps + step(g), Z_GLA // 1024)),
            pl.BlockSpec((tile, kw), lambda b, g: (step(g), 0)),
            pl.BlockSpec((tile, kw), lambda b, g: (step(g), 0)),
            pl.BlockSpec((LANES, kw), lambda b, g: (0, 0)),
            pl.BlockSpec((1, kw), lambda b, g: (0, 0)),
            pl.BlockSpec((1, kw, vw), lambda b, g: (b, 0, 0)),
        ],
        out_specs=[
            pl.BlockSpec((tile, vw), lambda b, g: (b * n_steps + step(g), 0)),
            pl.BlockSpec((1, kw, vw), lambda b, g: (b, 0, 0)),
        ],
        out_shape=[jax.ShapeDtypeStruct((B * seq, vw), F32), jax.ShapeDtypeStruct((B, kw, vw), F32)],
        scratch_shapes=[pltpu.VMEM((kw, vw), F32)],
        compiler_params=_cparams(("arbitrary", "arbitrary")), name="gla_rev" if reverse else "gla_fwd",
    )(z, cos, sin, gup, gb, s0)


def _rope_tables(T):
    t = np.arange(T)
    row, col = t // GRID_W, t % GRID_W
    nf = GLA_DK // 4
    inv = 1.0 / (ROPE_BASE ** (jnp.arange(nf, dtype=F32) / nf))
    j = np.arange(GLA_HEADS * GLA_DK)
    d = j % GLA_DK
    use_col = (d // (GLA_DK // 2)) == 1
    e = d % (GLA_DK // 2)
    fi = e % nf
    pos = jnp.where(jnp.asarray(use_col)[None, :], jnp.asarray(col, F32)[:, None], jnp.asarray(row, F32)[:, None])
    ang = pos * inv[fi][None, :]
    sign = jnp.asarray(np.where(e < nf, -1.0, 1.0), F32)[None, :]
    return jnp.cos(ang), jnp.sin(ang) * sign


CONV_PAD = 16
CONV_ROWS = 128


def _conv_body(z_ref, dw_ref, b_ref, g_ref, be_ref, o_ref, ypad_ref, *, seq):
    zeros = jnp.zeros((CONV_PAD, BRANCH_W), F32)
    ypad_ref[0:CONV_PAD, :] = zeros
    ypad_ref[seq + CONV_PAD:seq + 2 * CONV_PAD, :] = zeros

    def glu(i, carry):
        base = pl.multiple_of(i * CONV_ROWS, CONV_ROWS)
        a = z_ref[pl.ds(base, CONV_ROWS), 0:BRANCH_W]
        gate = z_ref[pl.ds(base, CONV_ROWS), BRANCH_W:2 * BRANCH_W]
        ypad_ref[pl.ds(base + CONV_PAD, CONV_ROWS), :] = a * jax.nn.sigmoid(gate)
        return carry

    lax.fori_loop(0, seq // CONV_ROWS, glu, 0)

    def tile(i, carry):
        base = pl.multiple_of(i * CONV_ROWS, CONV_ROWS)
        acc = jnp.zeros((CONV_ROWS, BRANCH_W), F32)
        win = ypad_ref[pl.ds(base, CONV_ROWS + 2 * CONV_PAD), :]
        for j in range(CONV_WIDTH):
            off = CONV_PAD - CONV_WIDTH // 2 + j
            acc = acc + win[off:off + CONV_ROWS] * dw_ref[j:j + 1, :]
        y = _layer_norm(acc + b_ref[...], g_ref[...], be_ref[...])
        o_ref[pl.ds(base, CONV_ROWS), :] = y * jax.nn.sigmoid(y)
        return carry

    lax.fori_loop(0, seq // CONV_ROWS, tile, 0)


def _conformer_conv(z, dw, b, g, be, *, B, seq, row0_blocks):
    vec = pl.BlockSpec((1, BRANCH_W), lambda i: (0, 0))
    return pl.pallas_call(
        functools.partial(_conv_body, seq=seq), grid=(B,),
        in_specs=[pl.BlockSpec((seq, 2 * BRANCH_W), lambda i: (row0_blocks + i, Z_CONV // (2 * BRANCH_W))),
                  pl.BlockSpec((CONV_WIDTH, BRANCH_W), lambda i: (0, 0)), vec, vec, vec],
        out_specs=pl.BlockSpec((seq, BRANCH_W), lambda i: (i, 0)),
        out_shape=jax.ShapeDtypeStruct((B * seq, BRANCH_W), F32),
        scratch_shapes=[pltpu.VMEM((seq + 2 * CONV_PAD, BRANCH_W), F32)],
        compiler_params=_cparams(("arbitrary",)), name="conformer_conv",
    )(z, dw, b, g, be)


def _sgu_body(z_ref, g_ref, b_ref, ws_ref, bs_ref, o_ref):
    zz = jax.nn.gelu(z_ref[...])
    u = zz[:, 0:BRANCH_W]
    v = _layer_norm(zz[:, BRANCH_W:2 * BRANCH_W], g_ref[...], b_ref[...])
    gms = _head_masks(BRANCH_W, BRANCH_W // SGU_GROUPS, SGU_GROUPS)
    for c in range(z_ref.shape[0] // SGU_CHUNK):
        sl = slice(c * SGU_CHUNK, (c + 1) * SGU_CHUNK)
        vb = v[sl].astype(BF16)
        s = bs_ref[...]
        for gi, gm in enumerate(gms):
            s = s + jnp.where(gm, jnp.dot(ws_ref[gi], vb, preferred_element_type=F32), 0.0)
        o_ref[sl, :] = u[sl] * s


def _spatial_gating(z, g, b, ws, bs_exp, n_tiles):
    vec = pl.BlockSpec((1, BRANCH_W), lambda i: (0, 0))
    return pl.pallas_call(
        _sgu_body, grid=(n_tiles,),
        in_specs=[pl.BlockSpec((TOK_TILE, 2 * BRANCH_W), lambda i: (i, Z_SGU // (2 * BRANCH_W))), vec, vec,
                  pl.BlockSpec(ws.shape, lambda i: (0, 0, 0)),
                  pl.BlockSpec((SGU_CHUNK, BRANCH_W), lambda i: (0, 0))],
        out_specs=pl.BlockSpec((TOK_TILE, BRANCH_W), lambda i: (i, 0)),
        out_shape=jax.ShapeDtypeStruct((n_tiles * TOK_TILE, BRANCH_W), F32),
        compiler_params=_cparams(("arbitrary",)), name="sgu",
    )(z, g, b, ws, bs_exp)


MERGE_TILE = 256


def _merge_body(mod_ref, gates_ref, yna_ref, of_ref, ob_ref, r_ref, ycv_ref, ysg_ref, x_ref, g1_ref, sc2_ref, sh2_ref,
                ng_ref, wb_ref, wo_ref, lg_ref, lb_ref, x1_ref, h2_ref):
    o = of_ref[...] + ob_ref[...]
    vw = GLA_HEADS * GLA_DV
    hr = lax.broadcasted_iota(I32, (vw, vw), 0) // GLA_DV
    hc = lax.broadcasted_iota(I32, (vw, vw), 1) // GLA_DV
    same_head = jnp.where(hr == hc, 1.0, 0.0).astype(BF16)
    ms = _dot_exact(same_head, o * o, lhs_is_x=True) * (1.0 / GLA_DV)
    r = r_ref[...]
    y_gla = o * lax.rsqrt(ms + LN_EPS) * ng_ref[...] * (r * jax.nn.sigmoid(r))
    ys = (yna_ref[...], y_gla, ycv_ref[...], ysg_ref[...])
    merged = None
    for i in range(4):
        gate = jax.nn.sigmoid(gates_ref[:, i * D_MODEL:(i + 1) * D_MODEL])
        term = gate * jnp.dot(ys[i].astype(BF16), wb_ref[i], preferred_element_type=F32)
        merged = term if merged is None else merged + term
    y = jnp.dot(merged.astype(BF16), wo_ref[...], preferred_element_type=F32)
    x1 = _layer_norm(ALPHA * x_ref[...] + g1_ref[0] * y, lg_ref[...], lb_ref[...])
    x1_ref[...] = x1
    h2_ref[...] = x1 * (1.0 + sc2_ref[0]) + sh2_ref[0]


def _merge(z, yna, of, ob, ycv, ysg, x, g1, sc2, sh2, ng, wb, wo, lg, lb, mod_idx, n_tiles):
    d = D_MODEL
    br = pl.BlockSpec((MERGE_TILE, BRANCH_W), lambda i, m: (i, 0))
    modv = pl.BlockSpec((1, 1, d), lambda i, m: (m[i], 0, 0))
    vec = pl.BlockSpec((1, d), lambda i, m: (0, 0))
    xt = pl.BlockSpec((MERGE_TILE, d), lambda i, m: (i, 0))
    grid_spec = pltpu.PrefetchScalarGridSpec(
        num_scalar_prefetch=1, grid=(n_tiles,),
        in_specs=[
            pl.BlockSpec((MERGE_TILE, 4 * d), lambda i, m: (i, 0)),
            br, br, br,
            pl.BlockSpec((MERGE_TILE, BRANCH_W), lambda i, m: (i, (Z_GLA + 512) // BRANCH_W)),
            br, br, xt, modv, modv, modv,
            pl.BlockSpec((1, BRANCH_W), lambda i, m: (0, 0)),
            pl.BlockSpec(wb.shape, lambda i, m: (0, 0, 0)),
            pl.BlockSpec(wo.shape, lambda i, m: (0, 0)),
            vec, vec,
        ],
        out_specs=[xt, xt],
    )
    return pl.pallas_call(
        _merge_body, grid_spec=grid_spec,
        out_shape=[jax.ShapeDtypeStruct((n_tiles * MERGE_TILE, d), F32)] * 2,
        compiler_params=_cparams(("arbitrary",)), name="merge",
    )(mod_idx, z, yna, of, ob, z, ycv, ysg, x, g1, sc2, sh2, ng, wb, wo, lg, lb)


TOPK_TILE = 256


def _top16(vals, payload=None):
    n_cand = vals.shape[0]
    pos_iota = lax.broadcasted_iota(I32, vals.shape, 0)
    out_v, out_p = [], []
    for _ in range(PEER_TOPK):
        m = jnp.max(vals, axis=0, keepdims=True)
        pos = jnp.min(jnp.where(vals == m, pos_iota, n_cand), axis=0, keepdims=True)
        hit = pos_iota == pos
        out_v.append(m)
        out_p.append(pos if payload is None else jnp.max(jnp.where(hit, payload, -1), axis=0, keepdims=True))
        vals = jnp.where(hit, -jnp.inf, vals)
    return jnp.concatenate(out_v, axis=0), jnp.concatenate(out_p, axis=0)


_PAIR_COUNTS = [PEER_TOPK // (i + 1) for i in range(PEER_TOPK)]


def _topk_body(q_ref, keys_ref, idx_ref, g_ref):
    half = PEER_DQ // 2
    n_tok = q_ref.shape[0]
    tops = []
    for s in range(2):
        qs = q_ref[:, s * half:(s + 1) * half].astype(BF16)
        sc = lax.dot_general(keys_ref[0, s], qs, _NT, preferred_element_type=F32)
        tops.append(_top16(sc))
    (a, ia), (b, ib) = tops
    cand = [a[i:i + 1] + b[0:n] for i, n in enumerate(_PAIR_COUNTS)]
    code = [ia[i:i + 1] * PEER_NKEYS + ib[0:n] for i, n in enumerate(_PAIR_COUNTS)]
    pad = -sum(_PAIR_COUNTS) % SUBLANES
    cand = jnp.concatenate(cand + [jnp.full((pad, n_tok), -jnp.inf, F32)], axis=0)
    code = jnp.concatenate(code + [jnp.zeros((pad, n_tok), I32)], axis=0)
    best, idx = _top16(cand, code)
    e = jnp.exp(best - jnp.max(best, axis=0, keepdims=True))
    g_ref[0] = e / jnp.sum(e, axis=0, keepdims=True)
    idx_ref[0] = idx


def _peer_topk(q, keys, n_tiles):
    ntok = n_tiles * TOPK_TILE
    out = pl.BlockSpec((1, PEER_TOPK, TOPK_TILE), lambda i, h: (h, 0, i))
    return pl.pallas_call(
        _topk_body, grid=(n_tiles, PEER_HEADS),
        in_specs=[pl.BlockSpec((TOPK_TILE, PEER_DQ), lambda i, h: (i, h)),
                  pl.BlockSpec((1, 2, PEER_NKEYS, PEER_DQ // 2), lambda i, h: (h, 0, 0, 0))],
        out_specs=[out, out],
        out_shape=[jax.ShapeDtypeStruct((PEER_HEADS, PEER_TOPK, ntok), I32),
                   jax.ShapeDtypeStruct((PEER_HEADS, PEER_TOPK, ntok), F32)],
        compiler_params=_cparams(("arbitrary", "arbitrary")), name="peer_topk",
    )(q, keys)


HALF_SUB = SUBLANES // 2
N_SEL = PEER_HEADS * PEER_TOPK
SEL_ROWS = SUBLANES * N_SEL
U_LOW_ROWS, U_HIGH_ROWS = (7, 5, 3, 1), (6, 4, 2, 0)
V_LOW_ROWS, V_HIGH_ROWS = (0, 2, 4, 6), (1, 3, 5, 7)


def _pack_table(tab, low_rows, high_rows):
    e = tab.shape[0]
    bits = lax.bitcast_convert_type(tab.astype(BF16), jnp.uint16).astype(jnp.uint32).reshape(e, SUBLANES, LANES)
    low = jnp.stack([bits[:, i] for i in low_rows], axis=1)
    high = jnp.stack([bits[:, i] for i in high_rows], axis=1)
    return lax.bitcast_convert_type(low | (high << 16), I32).reshape(e * HALF_SUB, LANES)


def _stage_rows(idx_ref, tab_ref, stage_ref, p):
    for k in range(N_SEL):
        off = pl.multiple_of(idx_ref[p, k], HALF_SUB)
        stage_ref[k * HALF_SUB:(k + 1) * HALF_SUB, :] = tab_ref[pl.ds(off, HALF_SUB), :]


def _hi_lo_rows(x):
    hi = x.astype(BF16).astype(F32)
    return jnp.concatenate([hi, x - hi], axis=0).astype(BF16)


def _two_stage_token_loop(n_tok, stage, compute):
    stage(0, 0)
    stage(1, 1)

    def pair(j, carry):
        p = 2 * j
        compute(p, 0)
        stage(jnp.minimum(p + 2, n_tok - 1), 0)
        compute(p + 1, 1)
        stage(jnp.minimum(p + 3, n_tok - 1), 1)
        return carry

    lax.fori_loop(0, n_tok // 2, pair, 0)


def _peer_u_body(idx_ref, h_ref, g_ref, tab_ref, w_ref, stage0_ref, stage1_ref, a_ref):
    stages = (stage0_ref, stage1_ref)

    def compute(p, slot):
        m = pltpu.bitcast(stages[slot][...], BF16)
        y = lax.dot_general(_hi_lo_rows(h_ref[p]), m, _NT, preferred_element_type=F32)
        y = y[0:SUBLANES] + y[SUBLANES:2 * SUBLANES]
        pieces = []
        for v in range(SEL_ROWS // LANES):
            z = pltpu.roll(y[:, v * LANES:(v + 1) * LANES], LANES - SUBLANES + 1, 1, stride=1, stride_axis=0)
            pieces.append(jnp.sum(z, axis=0, keepdims=True))
        a_ref[pl.ds(p, 1), :] = jnp.concatenate(pieces, axis=1)

    _two_stage_token_loop(h_ref.shape[0], lambda p, slot: _stage_rows(idx_ref, tab_ref, stages[slot], p), compute)
    w_ref[...] = g_ref[...] * jax.nn.gelu(a_ref[...])


def _peer_v_body(idx_ref, w_ref, tab_ref, o_ref, stage0_ref, stage1_ref):
    stages = (stage0_ref, stage1_ref)

    def compute(p, slot):
        m = pltpu.bitcast(stages[slot][...], BF16)
        wrow = w_ref[pl.ds(p, 1), :]
        pieces = []
        for v in range(SEL_ROWS // LANES):
            wb = jnp.broadcast_to(wrow[:, v * LANES:(v + 1) * LANES], (SUBLANES, LANES))
            pieces.append(pltpu.roll(wb, 0, 1, stride=1, stride_axis=0))
        out = jnp.dot(_hi_lo_rows(jnp.concatenate(pieces, axis=1)), m, preferred_element_type=F32)
        o_ref[p] = out[0:SUBLANES] + out[SUBLANES:2 * SUBLANES]

    _two_stage_token_loop(o_ref.shape[0], lambda p, slot: _stage_rows(idx_ref, tab_ref, stages[slot], p), compute)


def _table_spec(tab):
    return pl.BlockSpec(tab.shape, lambda i: (0, 0), pipeline_mode=pl.Buffered(1))


def _peer_u(idx, h3, g_sparse, tab, n_tiles):
    tb = GATHER_TILE
    wide = pl.BlockSpec((tb, SEL_ROWS), lambda i: (i, 0))
    return pl.pallas_call(
        _peer_u_body, grid=(n_tiles,),
        in_specs=[pl.BlockSpec((tb, N_SEL), lambda i: (i, 0), memory_space=pltpu.SMEM),
                  pl.BlockSpec((tb, SUBLANES, LANES), lambda i: (i, 0, 0)), wide, _table_spec(tab)],
        out_specs=wide,
        out_shape=jax.ShapeDtypeStruct((n_tiles * tb, SEL_ROWS), F32),
        scratch_shapes=[pltpu.VMEM((HALF_SUB * N_SEL, LANES), I32)] * 2 + [pltpu.VMEM((tb, SEL_ROWS), F32)],
        compiler_params=_cparams(("arbitrary",), vmem_mb=56), name="peer_u",
    )(idx, h3, g_sparse, tab)


def _peer_v(idx, w_sparse, tab, n_tiles):
    tb = GATHER_TILE
    return pl.pallas_call(
        _peer_v_body, grid=(n_tiles,),
        in_specs=[pl.BlockSpec((tb, N_SEL), lambda i: (i, 0), memory_space=pltpu.SMEM),
                  pl.BlockSpec((tb, SEL_ROWS), lambda i: (i, 0)), _table_spec(tab)],
        out_specs=pl.BlockSpec((tb, SUBLANES, LANES), lambda i: (i, 0, 0)),
        out_shape=jax.ShapeDtypeStruct((n_tiles * tb, SUBLANES, LANES), F32),
        scratch_shapes=[pltpu.VMEM((HALF_SUB * N_SEL, LANES), I32)] * 2,
        compiler_params=_cparams(("arbitrary",), vmem_mb=56), name="peer_v",
    )(idx, w_sparse, tab)


def _ln2_body(mod_ref, x_ref, f_ref, g2_ref, lg_ref, lb_ref, o_ref):
    o_ref[...] = _layer_norm(ALPHA * x_ref[...] + g2_ref[0] * f_ref[...], lg_ref[...], lb_ref[...])


def _ln2(x1, ffn, g2, lg, lb, mod_idx, n_tiles):
    d = D_MODEL
    xt = pl.BlockSpec((TOK_TILE, d), lambda i, m: (i, 0))
    vec = pl.BlockSpec((1, d), lambda i, m: (0, 0))
    grid_spec = pltpu.PrefetchScalarGridSpec(
        num_scalar_prefetch=1, grid=(n_tiles,),
        in_specs=[xt, xt, pl.BlockSpec((1, 1, d), lambda i, m: (m[i], 0, 0)), vec, vec],
        out_specs=xt,
    )
    return pl.pallas_call(
        _ln2_body, grid_spec=grid_spec,
        out_shape=jax.ShapeDtypeStruct((n_tiles * TOK_TILE, d), F32),
        compiler_params=_cparams(("arbitrary",)), name="ln2",
    )(mod_idx, x1, ffn, g2, lg, lb)


def _mod_index(B, T, L, tile, n_tiles):
    start = np.arange(n_tiles) * tile
    return jnp.asarray(np.where(start < B * T, start // T, B), I32)


def _pad_w_in(w_in):
    d = w_in.shape[0]
    na, gla, conv, sgu, gates = (w_in[:, 0:768], w_in[:, 768:1568], w_in[:, 1568:2080],
                                 w_in[:, 2080:2592], w_in[:, 2592:6688])
    z224 = jnp.zeros((d, 1024 - 800), w_in.dtype)
    z256 = jnp.zeros((d, 1024 - 768), w_in.dtype)
    return jnp.concatenate([gates, conv, sgu, gla, z224, na, z256], axis=1).astype(BF16)


def kernel(x, c, ctx, c_ctx, ada_w, ada_b, w_in, na_rpb, gla_gate_up, gla_gate_b, gla_norm_g, conv_dw, conv_b,
           conv_ln_g, conv_ln_b, sgu_ln_g, sgu_ln_b, sgu_ws, sgu_bs, w_branch, w_out, ln1_g, ln1_b, peer_wq,
           peer_keys, peer_u, peer_v, ln2_g, ln2_b):
    B, T, D = x.shape
    L = ctx.shape[1]
    depth = ada_w.shape[0]
    n_lat, n_ctx = B * T, B * L
    ntok = n_lat + n_ctx
    kw, vw = GLA_HEADS * GLA_DK, GLA_HEADS * GLA_DV

    xa = jnp.concatenate([x.reshape(n_lat, D), ctx.reshape(n_ctx, D)], axis=0)
    cos_l, sin_l = _rope_tables(T)
    cos_c, sin_c = jnp.ones((L, kw), F32), jnp.zeros((L, kw), F32)
    s_zero = jnp.zeros((B, kw, vw), F32)
    n_mod = -(-(B + 1) // SUBLANES) * SUBLANES
    cpad = jnp.concatenate([c, c_ctx[None, :], jnp.zeros((n_mod - B - 1, D), F32)], axis=0)

    for l in range(depth):
        need_ctx = l < depth - 1
        n_act = ntok if need_ctx else n_lat
        mod = _ada(cpad, ada_w[l], ada_b[l][None, :])
        sh1, sc1, g1, sh2, sc2, g2 = [m[:, None, :] for m in jnp.split(mod, 6, axis=-1)]

        z = _modmm(xa, sc1, sh1, _pad_w_in(w_in[l]), _mod_index(B, T, L, TOK_TILE, ntok // TOK_TILE),
                   ntok // TOK_TILE, 1024)

        y_na = _na_attention(z, _na_bias_table(na_rpb[l]), B, T, L)

        gups = []
        for d in range(2):
            gu = jnp.zeros((LANES, kw), F32).at[d * GLA_GATE_RANK:(d + 1) * GLA_GATE_RANK].set(gla_gate_up[l, d])
            gups.append(gu.astype(BF16))
        gbs = [gla_gate_b[l, d][None, :] for d in range(2)]
        ctx_blocks = n_lat // L
        oc_f, sc_f = _gla_scan(z, cos_c, sin_c, gups[0], gbs[0], s_zero, B=B, seq=L, row0_blocks=ctx_blocks,
                               tile=L, reverse=False)
        oc_b, sc_b = _gla_scan(z, cos_c, sin_c, gups[1], gbs[1], s_zero, B=B, seq=L, row0_blocks=ctx_blocks,
                               tile=L, reverse=True)
        o_f, _ = _gla_scan(z, cos_l, sin_l, gups[0], gbs[0], sc_f, B=B, seq=T, row0_blocks=0, tile=TOK_TILE,
                           reverse=False)
        o_b, _ = _gla_scan(z, cos_l, sin_l, gups[1], gbs[1], sc_b, B=B, seq=T, row0_blocks=0, tile=TOK_TILE,
                           reverse=True)

        cv_args = (conv_dw[l], conv_b[l][None, :], conv_ln_g[l][None, :], conv_ln_b[l][None, :])
        y_cv = _conformer_conv(z, *cv_args, B=B, seq=T, row0_blocks=0)
        bs_exp = jnp.repeat(sgu_bs[l].T, BRANCH_W // SGU_GROUPS, axis=1)
        y_sg = _spatial_gating(z, sgu_ln_g[l][None, :], sgu_ln_b[l][None, :], sgu_ws[l].astype(BF16), bs_exp,
                               n_act // TOK_TILE)

        if need_ctx:
            y_na = jnp.concatenate([y_na, _ctx_attention(z, B, T, L)], axis=0)
            o_f = jnp.concatenate([o_f, oc_f], axis=0)
            o_b = jnp.concatenate([o_b, oc_b], axis=0)
            y_cv = jnp.concatenate([y_cv, _conformer_conv(z, *cv_args, B=B, seq=L, row0_blocks=ctx_blocks)], axis=0)

        x1, h2 = _merge(z, y_na, o_f, o_b, y_cv, y_sg, xa, g1, sc2, sh2, gla_norm_g[l].reshape(1, vw),
                        w_branch[l].astype(BF16), w_out[l].astype(BF16), ln1_g[l][None, :], ln1_b[l][None, :],
                        _mod_index(B, T, L, MERGE_TILE, n_act // MERGE_TILE), n_act // MERGE_TILE)

        q = _mm(h2, peer_wq[l].astype(BF16), n_act // TOK_TILE, 1024)
        idx_t, g_t = _peer_topk(q, peer_keys[l].astype(BF16), n_act // TOPK_TILE)
        idx = idx_t.reshape(N_SEL, n_act).T * HALF_SUB
        gate = g_t.reshape(N_SEL, n_act).T
        g_sparse = jnp.pad(gate[:, :, None], ((0, 0), (0, 0), (0, SUBLANES - 1))).reshape(n_act, SEL_ROWS)
        w_sparse = _peer_u(idx, h2.reshape(n_act, SUBLANES, LANES), g_sparse,
                           _pack_table(peer_u[l], U_LOW_ROWS, U_HIGH_ROWS), n_act // GATHER_TILE)
        ffn = _peer_v(idx, w_sparse, _pack_table(peer_v[l], V_LOW_ROWS, V_HIGH_ROWS),
                      n_act // GATHER_TILE).reshape(n_act, D)

        xa = _ln2(x1, ffn, g2, ln2_g[l][None, :], ln2_b[l][None, :],
                  _mod_index(B, T, L, TOK_TILE, n_act // TOK_TILE), n_act // TOK_TILE)

    return xa[:n_lat].reshape(B, T, D)
```

```python
import functools

import numpy as np
import jax
import jax.numpy as jnp
from jax import lax
from jax.experimental import pallas as pl
from jax.experimental.pallas import tpu as pltpu

F32 = jnp.float32
BF16 = jnp.bfloat16
I32 = jnp.int32

D_MODEL = 1024
GRID_W = 64
BRANCH_W = 256
NA_HEADS = 4
NA_HEAD_DIM = 64
NA_WIN_H = 8
NA_WIN_W = 16
GLA_HEADS = 4
GLA_DV = 64
GLA_DK = 32
GLA_GATE_RANK = 16
GLA_TAU = 16.0
GLA_CHUNK = 64
ROPE_BASE = 100.0
CONV_WIDTH = 31
SGU_GROUPS = 4
SGU_CHUNK = 128
PEER_HEADS = 8
PEER_NKEYS = 128
PEER_DQ = 256
PEER_TOPK = 16
DEPTH = 2
ALPHA = (2 * DEPTH) ** 0.25
NEG_INF = -1e30
LN_EPS = 1e-6

SUBLANES = 8
LANES = 128

Z_GATES = 0
Z_CONV = 4096
Z_SGU = 4608
Z_GLA = 5120
Z_NA = 6144
Z_COLS = 7168

TOK_TILE = 512
GATHER_TILE = 128

_NT = (((1,), (1,)), ((), ()))
_TN = (((0,), (0,)), ((), ()))


def _cparams(sem, vmem_mb=48):
    return pltpu.CompilerParams(dimension_semantics=sem, vmem_limit_bytes=vmem_mb * 1024 * 1024)


def _split3(x):
    hi = x.astype(BF16)
    r1 = x - hi.astype(F32)
    mid = r1.astype(BF16)
    lo = (r1 - mid.astype(F32)).astype(BF16)
    return hi, mid, lo


def _dot_exact(a_bf16, x, dims=None, lhs_is_x=False):
    out = None
    for p in _split3(x):
        if dims is None:
            t = jnp.dot(p, a_bf16, preferred_element_type=F32) if lhs_is_x else jnp.dot(a_bf16, p, preferred_element_type=F32)
        else:
            t = lax.dot_general(p, a_bf16, dims, preferred_element_type=F32) if lhs_is_x else lax.dot_general(a_bf16, p, dims, preferred_element_type=F32)
        out = t if out is None else out + t
    return out


def _layer_norm(x, g, b):
    mu = jnp.mean(x, axis=-1, keepdims=True)
    xc = x - mu
    var = jnp.mean(xc * xc, axis=-1, keepdims=True)
    return xc * lax.rsqrt(var + LN_EPS) * g + b


ADA_TILE = 512


def _ada_body(c_ref, w_ref, b_ref, o_ref):
    c = c_ref[...]
    cs = c * jax.nn.sigmoid(c)
    o_ref[...] = jnp.dot(cs, w_ref[...], preferred_element_type=F32, precision=lax.Precision.HIGHEST) + b_ref[...]


def _ada(cpad, w, b):
    rows, d = cpad.shape
    n = w.shape[1]
    return pl.pallas_call(
        _ada_body, grid=(n // ADA_TILE,),
        in_specs=[pl.BlockSpec((rows, d), lambda j: (0, 0)),
                  pl.BlockSpec((d, ADA_TILE), lambda j: (0, j)),
                  pl.BlockSpec((1, ADA_TILE), lambda j: (0, j))],
        out_specs=pl.BlockSpec((rows, ADA_TILE), lambda j: (0, j)),
        out_shape=jax.ShapeDtypeStruct((rows, n), F32),
        compiler_params=_cparams(("arbitrary",)), name="ada",
    )(cpad, w, b)


def _modmm_body(mod_ref, x_ref, sc_ref, sh_ref, w_ref, o_ref):
    h = x_ref[...] * (1.0 + sc_ref[0]) + sh_ref[0]
    o_ref[...] = jnp.dot(h.astype(BF16), w_ref[...], preferred_element_type=F32)


def _mm_body(x_ref, w_ref, o_ref):
    o_ref[...] = jnp.dot(x_ref[...].astype(BF16), w_ref[...], preferred_element_type=F32)


def _modmm(x, sc, sh, w, mod_idx, n_tiles, tn):
    d = x.shape[1]
    n = w.shape[1]
    grid_spec = pltpu.PrefetchScalarGridSpec(
        num_scalar_prefetch=1,
        grid=(n // tn, n_tiles),
        in_specs=[
            pl.BlockSpec((TOK_TILE, d), lambda j, i, m: (i, 0)),
            pl.BlockSpec((1, 1, d), lambda j, i, m: (m[i], 0, 0)),
            pl.BlockSpec((1, 1, d), lambda j, i, m: (m[i], 0, 0)),
            pl.BlockSpec((d, tn), lambda j, i, m: (0, j)),
        ],
        out_specs=pl.BlockSpec((TOK_TILE, tn), lambda j, i, m: (i, j)),
    )
    return pl.pallas_call(
        _modmm_body, grid_spec=grid_spec,
        out_shape=jax.ShapeDtypeStruct((n_tiles * TOK_TILE, n), F32),
        compiler_params=_cparams(("arbitrary", "arbitrary")), name="modmm",
    )(mod_idx, x, sc, sh, w)


def _mm(x, w, n_tiles, tn):
    d = x.shape[1]
    n = w.shape[1]
    return pl.pallas_call(
        _mm_body, grid=(n // tn, n_tiles),
        in_specs=[pl.BlockSpec((TOK_TILE, d), lambda j, i: (i, 0)),
                  pl.BlockSpec((d, tn), lambda j, i: (0, j))],
        out_specs=pl.BlockSpec((TOK_TILE, tn), lambda j, i: (i, j)),
        out_shape=jax.ShapeDtypeStruct((n_tiles * TOK_TILE, n), F32),
        compiler_params=_cparams(("arbitrary", "arbitrary")), name="peer_q",
    )(x, w)


def _head_masks(width, per_head, heads):
    lane = lax.broadcasted_iota(I32, (1, width), 1)
    return [(lane >= h * per_head) & (lane < (h + 1) * per_head) for h in range(heads)]


def _na_body(q_ref, k_ref, v_ref, kc_ref, vc_ref, bias_ref, o_ref, *, rows):
    r = pl.program_id(1)
    r0 = jnp.clip(r - NA_WIN_H // 2, 0, rows - NA_WIN_H)
    delta = r - r0
    start = pl.multiple_of(r0 * GRID_W, GRID_W)
    nwin = NA_WIN_H * GRID_W
    kwin = k_ref[pl.ds(start, nwin), :].astype(BF16)
    vwin = v_ref[pl.ds(start, nwin), :].astype(BF16)
    kc = kc_ref[...].astype(BF16)
    vc = vc_ref[...].astype(BF16)
    q = q_ref[...] * (NA_HEAD_DIM ** -0.5)
    out = jnp.zeros(q.shape, F32)
    for h, hm in enumerate(_head_masks(BRANCH_W, NA_HEAD_DIM, NA_HEADS)):
        qh = jnp.where(hm, q, 0.0).astype(BF16)
        s = lax.dot_general(qh, kwin, _NT, preferred_element_type=F32) + bias_ref[h, delta]
        sc = lax.dot_general(qh, kc, _NT, preferred_element_type=F32)
        m = jnp.maximum(jnp.max(s, axis=1, keepdims=True), jnp.max(sc, axis=1, keepdims=True))
        e = jnp.exp(s - m)
        ec = jnp.exp(sc - m)
        den = jnp.sum(e, axis=1, keepdims=True) + jnp.sum(ec, axis=1, keepdims=True)
        oh = (jnp.dot(e.astype(BF16), vwin, preferred_element_type=F32)
              + jnp.dot(ec.astype(BF16), vc, preferred_element_type=F32))
        out = out + jnp.where(hm, oh / den, 0.0)
    o_ref[...] = out


def _na_bias_table(rpb):
    colv = np.arange(GRID_W)
    c0 = np.clip(colv - NA_WIN_W // 2, 0, GRID_W - NA_WIN_W)
    in_win = (colv[None, :] >= c0[:, None]) & (colv[None, :] < c0[:, None] + NA_WIN_W)
    edge = GRID_W - NA_WIN_W
    padded = jnp.pad(rpb, ((0, 0), (0, 0), (edge, edge)), mode="edge")
    cols = jnp.stack([padded[:, :, GRID_W - 1 - q:2 * GRID_W - 1 - q] for q in range(GRID_W)], axis=2)
    cols = jnp.where(jnp.asarray(in_win)[None, None], cols, NEG_INF)
    b = jnp.stack([cols[:, NA_WIN_H - 1 - d:2 * NA_WIN_H - 1 - d] for d in range(NA_WIN_H)], axis=1)
    b = b.transpose(0, 1, 3, 2, 4)
    return b.reshape(NA_HEADS, NA_WIN_H, GRID_W, NA_WIN_H * GRID_W).astype(F32)


def _na_attention(z, bias, B, T, L):
    rows = T // GRID_W
    cq, ck, cv = Z_NA // BRANCH_W, Z_NA // BRANCH_W + 1, Z_NA // BRANCH_W + 2
    ctx0 = (B * T) // L
    return pl.pallas_call(
        functools.partial(_na_body, rows=rows), grid=(B, rows),
        in_specs=[
            pl.BlockSpec((GRID_W, BRANCH_W), lambda b, r: (b * rows + r, cq)),
            pl.BlockSpec((T, BRANCH_W), lambda b, r: (b, ck)),
            pl.BlockSpec((T, BRANCH_W), lambda b, r: (b, cv)),
            pl.BlockSpec((L, BRANCH_W), lambda b, r: (ctx0 + b, ck)),
            pl.BlockSpec((L, BRANCH_W), lambda b, r: (ctx0 + b, cv)),
            pl.BlockSpec(bias.shape, lambda b, r: (0, 0, 0, 0)),
        ],
        out_specs=pl.BlockSpec((GRID_W, BRANCH_W), lambda b, r: (b * rows + r, 0)),
        out_shape=jax.ShapeDtypeStruct((B * T, BRANCH_W), F32),
        compiler_params=_cparams(("arbitrary", "arbitrary")), name="na_attn",
    )(z, z, z, z, z, bias)


def _ctx_attn_body(q_ref, k_ref, v_ref, o_ref):
    k = k_ref[...].astype(BF16)
    v = v_ref[...].astype(BF16)
    q = q_ref[...] * (NA_HEAD_DIM ** -0.5)
    out = jnp.zeros(q.shape, F32)
    for hm in _head_masks(BRANCH_W, NA_HEAD_DIM, NA_HEADS):
        qh = jnp.where(hm, q, 0.0).astype(BF16)
        s = lax.dot_general(qh, k, _NT, preferred_element_type=F32)
        m = jnp.max(s, axis=1, keepdims=True)
        e = jnp.exp(s - m)
        den = jnp.sum(e, axis=1, keepdims=True)
        oh = jnp.dot(e.astype(BF16), v, preferred_element_type=F32)
        out = out + jnp.where(hm, oh / den, 0.0)
    o_ref[...] = out


def _ctx_attention(z, B, T, L):
    cq = Z_NA // BRANCH_W
    ctx0 = (B * T) // L
    return pl.pallas_call(
        _ctx_attn_body, grid=(B,),
        in_specs=[pl.BlockSpec((L, BRANCH_W), lambda b, c=c: (ctx0 + b, cq + c)) for c in range(3)],
        out_specs=pl.BlockSpec((L, BRANCH_W), lambda b: (b, 0)),
        out_shape=jax.ShapeDtypeStruct((B * L, BRANCH_W), F32),
        compiler_params=_cparams(("arbitrary",)), name="ctx_attn",
    )(z, z, z)


def _gla_body(z_ref, cos_ref, sin_ref, gup_ref, gb_ref, s0_ref, o_ref, sfin_ref, state_ref, *, reverse, n_steps):
    g = pl.program_id(1)

    @pl.when(g == 0)
    def _():
        state_ref[...] = s0_ref[0]

    kw = GLA_HEADS * GLA_DK
    vw = GLA_HEADS * GLA_DV
    q = z_ref[:, 0:kw] * (GLA_DK ** -0.5)
    k = z_ref[:, kw:2 * kw]
    v = z_ref[:, 2 * kw:2 * kw + vw]
    lo = z_ref[:, 2 * kw + 2 * vw:2 * kw + 2 * vw + LANES]
    cos = cos_ref[...]
    sin = sin_ref[...]
    lane = lax.broadcasted_iota(I32, (1, kw), 1)
    first = (lane % (GLA_DK // 2)) < (GLA_DK // 4)

    def rope(x):
        partner = jnp.where(first, pltpu.roll(x, kw - GLA_DK // 4, 1), pltpu.roll(x, GLA_DK // 4, 1))
        return x * cos + partner * sin

    q = rope(q)
    k = rope(k)
    logits = jnp.dot(lo.astype(BF16), gup_ref[...], preferred_element_type=F32) + gb_ref[...]
    la = (jnp.minimum(logits, 0.0) - jnp.log1p(jnp.exp(-jnp.abs(logits)))) / GLA_TAU

    C = GLA_CHUNK
    ri = lax.broadcasted_iota(I32, (C, C), 0)
    ci = lax.broadcasted_iota(I32, (C, C), 1)
    tri = (ri <= ci) if reverse else (ri >= ci)
    cum = jnp.where(tri, 1.0, 0.0).astype(BF16)
    tri4 = jnp.concatenate([tri] * GLA_HEADS, axis=0)
    hm_k = _head_masks(kw, GLA_DK, GLA_HEADS)
    hm_v = _head_masks(vw, GLA_DV, GLA_HEADS)
    srow = lax.broadcasted_iota(I32, (kw, vw), 0) // GLA_DK
    scol = lax.broadcasted_iota(I32, (kw, vw), 1) // GLA_DV
    blockmask = srow == scol
    ones_cv = jnp.ones((C, vw), BF16)

    S = state_ref[...]
    n_chunks = z_ref.shape[0] // C
    order = range(n_chunks - 1, -1, -1) if reverse else range(n_chunks)
    for c in order:
        sl = slice(c * C, (c + 1) * C)
        la_c = la[sl]
        b = _dot_exact(cum, la_c)
        bl = b[0:1] if reverse else b[C - 1:C]
        qs = q[sl] * jnp.exp(b)
        ks = k[sl] * jnp.exp(-b)
        ke = k[sl] * jnp.exp(bl - b)
        vb = v[sl].astype(BF16)
        qs_b = qs.astype(BF16)
        qstack = jnp.concatenate([jnp.where(hm, qs, 0.0) for hm in hm_k], axis=0).astype(BF16)
        a = lax.dot_general(qstack, ks.astype(BF16), _NT, preferred_element_type=F32)
        a = jnp.where(tri4, a, 0.0)
        o_stack = jnp.dot(a.astype(BF16), vb, preferred_element_type=F32)
        o_c = jnp.dot(qs_b, S.astype(BF16), preferred_element_type=F32)
        for h, hm in enumerate(hm_v):
            o_c = o_c + jnp.where(hm, o_stack[h * C:(h + 1) * C], 0.0)
        o_ref[sl, :] = o_c
        u = lax.dot_general(ke.astype(BF16), vb, _TN, preferred_element_type=F32)
        dcol = _dot_exact(ones_cv, la_c, dims=_TN, lhs_is_x=True)
        S = jnp.exp(dcol) * S + jnp.where(blockmask, u, 0.0)
    state_ref[...] = S

    @pl.when(g == n_steps - 1)
    def _():
        sfin_ref[0] = S


def _gla_scan(z, cos, sin, gup, gb, s0, *, B, seq, row0_blocks, tile, reverse):
    n_steps = seq // tile
    kw = GLA_HEADS * GLA_DK
    vw = GLA_HEADS * GLA_DV

    def step(g):
        return n_steps - 1 - g if reverse else g

    return pl.pallas_call(
        functools.partial(_gla_body, reverse=reverse, n_steps=n_steps), grid=(B, n_steps),
        in_specs=[
            pl.BlockSpec((tile, 1024), lambda b, g: (row0_blocks + b * n_steps + step(g), Z_GLA // 1024)),
            pl.BlockSpec((tile, kw), lambda b, g: (step(g), 0)),
            pl.BlockSpec((tile, kw), lambda b, g: (step(g), 0)),
            pl.BlockSpec((LANES, kw), lambda b, g: (0, 0)),
            pl.BlockSpec((1, kw), lambda b, g: (0, 0)),
            pl.BlockSpec((1, kw, vw), lambda b, g: (b, 0, 0)),
        ],
        out_specs=[
            pl.BlockSpec((tile, vw), lambda b, g: (b * n_steps + step(g), 0)),
            pl.BlockSpec((1, kw, vw), lambda b, g: (b, 0, 0)),
        ],
        out_shape=[jax.ShapeDtypeStruct((B * seq, vw), F32), jax.ShapeDtypeStruct((B, kw, vw), F32)],
        scratch_shapes=[pltpu.VMEM((kw, vw), F32)],
        compiler_params=_cparams(("arbitrary", "arbitrary")), name="gla_rev" if reverse else "gla_fwd",
    )(z, cos, sin, gup, gb, s0)


def _rope_tables(T):
    t = np.arange(T)
    row, col = t // GRID_W, t % GRID_W
    nf = GLA_DK // 4
    inv = 1.0 / (ROPE_BASE ** (jnp.arange(nf, dtype=F32) / nf))
    j = np.arange(GLA_HEADS * GLA_DK)
    d = j % GLA_DK
    use_col = (d // (GLA_DK // 2)) == 1
    e = d % (GLA_DK // 2)
    fi = e % nf
    pos = jnp.where(jnp.asarray(use_col)[None, :], jnp.asarray(col, F32)[:, None], jnp.asarray(row, F32)[:, None])
    ang = pos * inv[fi][None, :]
    sign = jnp.asarray(np.where(e < nf, -1.0, 1.0), F32)[None, :]
    return jnp.cos(ang), jnp.sin(ang) * sign


CONV_PAD = 16
CONV_ROWS = 128


def _conv_body(z_ref, dw_ref, b_ref, g_ref, be_ref, o_ref, ypad_ref, *, seq):
    zeros = jnp.zeros((CONV_PAD, BRANCH_W), F32)
    ypad_ref[0:CONV_PAD, :] = zeros
    ypad_ref[seq + CONV_PAD:seq + 2 * CONV_PAD, :] = zeros

    def glu(i, carry):
        base = pl.multiple_of(i * CONV_ROWS, CONV_ROWS)
        a = z_ref[pl.ds(base, CONV_ROWS), 0:BRANCH_W]
        gate = z_ref[pl.ds(base, CONV_ROWS), BRANCH_W:2 * BRANCH_W]
        ypad_ref[pl.ds(base + CONV_PAD, CONV_ROWS), :] = a * jax.nn.sigmoid(gate)
        return carry

    lax.fori_loop(0, seq // CONV_ROWS, glu, 0)

    def tile(i, carry):
        base = pl.multiple_of(i * CONV_ROWS, CONV_ROWS)
        acc = jnp.zeros((CONV_ROWS, BRANCH_W), F32)
        win = ypad_ref[pl.ds(base, CONV_ROWS + 2 * CONV_PAD), :]
        for j in range(CONV_WIDTH):
            off = CONV_PAD - CONV_WIDTH // 2 + j
            acc = acc + win[off:off + CONV_ROWS] * dw_ref[j:j + 1, :]
        y = _layer_norm(acc + b_ref[...], g_ref[...], be_ref[...])
        o_ref[pl.ds(base, CONV_ROWS), :] = y * jax.nn.sigmoid(y)
        return carry

    lax.fori_loop(0, seq // CONV_ROWS, tile, 0)


def _conformer_conv(z, dw, b, g, be, *, B, seq, row0_blocks):
    vec = pl.BlockSpec((1, BRANCH_W), lambda i: (0, 0))
    return pl.pallas_call(
        functools.partial(_conv_body, seq=seq), grid=(B,),
        in_specs=[pl.BlockSpec((seq, 2 * BRANCH_W), lambda i: (row0_blocks + i, Z_CONV // (2 * BRANCH_W))),
                  pl.BlockSpec((CONV_WIDTH, BRANCH_W), lambda i: (0, 0)), vec, vec, vec],
        out_specs=pl.BlockSpec((seq, BRANCH_W), lambda i: (i, 0)),
        out_shape=jax.ShapeDtypeStruct((B * seq, BRANCH_W), F32),
        scratch_shapes=[pltpu.VMEM((seq + 2 * CONV_PAD, BRANCH_W), F32)],
        compiler_params=_cparams(("arbitrary",)), name="conformer_conv",
    )(z, dw, b, g, be)


def _sgu_body(z_ref, g_ref, b_ref, ws_ref, bs_ref, o_ref):
    zz = jax.nn.gelu(z_ref[...])
    u = zz[:, 0:BRANCH_W]
    v = _layer_norm(zz[:, BRANCH_W:2 * BRANCH_W], g_ref[...], b_ref[...])
    gms = _head_masks(BRANCH_W, BRANCH_W // SGU_GROUPS, SGU_GROUPS)
    for c in range(z_ref.shape[0] // SGU_CHUNK):
        sl = slice(c * SGU_CHUNK, (c + 1) * SGU_CHUNK)
        vb = v[sl].astype(BF16)
        s = bs_ref[...]
        for gi, gm in enumerate(gms):
            s = s + jnp.where(gm, jnp.dot(ws_ref[gi], vb, preferred_element_type=F32), 0.0)
        o_ref[sl, :] = u[sl] * s


def _spatial_gating(z, g, b, ws, bs_exp, n_tiles):
    vec = pl.BlockSpec((1, BRANCH_W), lambda i: (0, 0))
    return pl.pallas_call(
        _sgu_body, grid=(n_tiles,),
        in_specs=[pl.BlockSpec((TOK_TILE, 2 * BRANCH_W), lambda i: (i, Z_SGU // (2 * BRANCH_W))), vec, vec,
                  pl.BlockSpec(ws.shape, lambda i: (0, 0, 0)),
                  pl.BlockSpec((SGU_CHUNK, BRANCH_W), lambda i: (0, 0))],
        out_specs=pl.BlockSpec((TOK_TILE, BRANCH_W), lambda i: (i, 0)),
        out_shape=jax.ShapeDtypeStruct((n_tiles * TOK_TILE, BRANCH_W), F32),
        compiler_params=_cparams(("arbitrary",)), name="sgu",
    )(z, g, b, ws, bs_exp)


MERGE_TILE = 256


def _merge_body(mod_ref, gates_ref, yna_ref, of_ref, ob_ref, r_ref, ycv_ref, ysg_ref, x_ref, g1_ref, sc2_ref, sh2_ref,
                ng_ref, wb_ref, wo_ref, lg_ref, lb_ref, x1_ref, h2_ref):
    o = of_ref[...] + ob_ref[...]
    vw = GLA_HEADS * GLA_DV
    hr = lax.broadcasted_iota(I32, (vw, vw), 0) // GLA_DV
    hc = lax.broadcasted_iota(I32, (vw, vw), 1) // GLA_DV
    same_head = jnp.where(hr == hc, 1.0, 0.0).astype(BF16)
    ms = _dot_exact(same_head, o * o, lhs_is_x=True) * (1.0 / GLA_DV)
    r = r_ref[...]
    y_gla = o * lax.rsqrt(ms + LN_EPS) * ng_ref[...] * (r * jax.nn.sigmoid(r))
    ys = (yna_ref[...], y_gla, ycv_ref[...], ysg_ref[...])
    merged = None
    for i in range(4):
        gate = jax.nn.sigmoid(gates_ref[:, i * D_MODEL:(i + 1) * D_MODEL])
        term = gate * jnp.dot(ys[i].astype(BF16), wb_ref[i], preferred_element_type=F32)
        merged = term if merged is None else merged + term
    y = jnp.dot(merged.astype(BF16), wo_ref[...], preferred_element_type=F32)
    x1 = _layer_norm(ALPHA * x_ref[...] + g1_ref[0] * y, lg_ref[...], lb_ref[...])
    x1_ref[...] = x1
    h2_ref[...] = x1 * (1.0 + sc2_ref[0]) + sh2_ref[0]


def _merge(z, yna, of, ob, ycv, ysg, x, g1, sc2, sh2, ng, wb, wo, lg, lb, mod_idx, n_tiles):
    d = D_MODEL
    br = pl.BlockSpec((MERGE_TILE, BRANCH_W), lambda i, m: (i, 0))
    modv = pl.BlockSpec((1, 1, d), lambda i, m: (m[i], 0, 0))
    vec = pl.BlockSpec((1, d), lambda i, m: (0, 0))
    xt = pl.BlockSpec((MERGE_TILE, d), lambda i, m: (i, 0))
    grid_spec = pltpu.PrefetchScalarGridSpec(
        num_scalar_prefetch=1, grid=(n_tiles,),
        in_specs=[
            pl.BlockSpec((MERGE_TILE, 4 * d), lambda i, m: (i, 0)),
            br, br, br,
            pl.BlockSpec((MERGE_TILE, BRANCH_W), lambda i, m: (i, (Z_GLA + 512) // BRANCH_W)),
            br, br, xt, modv, modv, modv,
            pl.BlockSpec((1, BRANCH_W), lambda i, m: (0, 0)),
            pl.BlockSpec(wb.shape, lambda i, m: (0, 0, 0)),
            pl.BlockSpec(wo.shape, lambda i, m: (0, 0)),
            vec, vec,
        ],
        out_specs=[xt, xt],
    )
    return pl.pallas_call(
        _merge_body, grid_spec=grid_spec,
        out_shape=[jax.ShapeDtypeStruct((n_tiles * MERGE_TILE, d), F32)] * 2,
        compiler_params=_cparams(("arbitrary",)), name="merge",
    )(mod_idx, z, yna, of, ob, z, ycv, ysg, x, g1, sc2, sh2, ng, wb, wo, lg, lb)


TOPK_TILE = 256


def _top16(vals, payload=None):
    n_cand = vals.shape[0]
    pos_iota = lax.broadcasted_iota(I32, vals.shape, 0)
    out_v, out_p = [], []
    for _ in range(PEER_TOPK):
        m = jnp.max(vals, axis=0, keepdims=True)
        pos = jnp.min(jnp.where(vals == m, pos_iota, n_cand), axis=0, keepdims=True)
        hit = pos_iota == pos
        out_v.append(m)
        out_p.append(pos if payload is None else jnp.max(jnp.where(hit, payload, -1), axis=0, keepdims=True))
        vals = jnp.where(hit, -jnp.inf, vals)
    return jnp.concatenate(out_v, axis=0), jnp.concatenate(out_p, axis=0)


_PAIR_COUNTS = [PEER_TOPK // (i + 1) for i in range(PEER_TOPK)]


def _topk_body(q_ref, keys_ref, idx_ref, g_ref):
    half = PEER_DQ // 2
    n_tok = q_ref.shape[0]
    tops = []
    for s in range(2):
        qs = q_ref[:, s * half:(s + 1) * half].astype(BF16)
        sc = lax.dot_general(keys_ref[0, s], qs, _NT, preferred_element_type=F32)
        tops.append(_top16(sc))
    (a, ia), (b, ib) = tops
    cand = [a[i:i + 1] + b[0:n] for i, n in enumerate(_PAIR_COUNTS)]
    code = [ia[i:i + 1] * PEER_NKEYS + ib[0:n] for i, n in enumerate(_PAIR_COUNTS)]
    pad = -sum(_PAIR_COUNTS) % SUBLANES
    cand = jnp.concatenate(cand + [jnp.full((pad, n_tok), -jnp.inf, F32)], axis=0)
    code = jnp.concatenate(code + [jnp.zeros((pad, n_tok), I32)], axis=0)
    best, idx = _top16(cand, code)
    e = jnp.exp(best - jnp.max(best, axis=0, keepdims=True))
    g_ref[0] = e / jnp.sum(e, axis=0, keepdims=True)
    idx_ref[0] = idx


def _peer_topk(q, keys, n_tiles):
    ntok = n_tiles * TOPK_TILE
    out = pl.BlockSpec((1, PEER_TOPK, TOPK_TILE), lambda i, h: (h, 0, i))
    return pl.pallas_call(
        _topk_body, grid=(n_tiles, PEER_HEADS),
        in_specs=[pl.BlockSpec((TOPK_TILE, PEER_DQ), lambda i, h: (i, h)),
                  pl.BlockSpec((1, 2, PEER_NKEYS, PEER_DQ // 2), lambda i, h: (h, 0, 0, 0))],
        out_specs=[out, out],
        out_shape=[jax.ShapeDtypeStruct((PEER_HEADS, PEER_TOPK, ntok), I32),
                   jax.ShapeDtypeStruct((PEER_HEADS, PEER_TOPK, ntok), F32)],
        compiler_params=_cparams(("arbitrary", "arbitrary")), name="peer_topk",
    )(q, keys)


HALF_SUB = SUBLANES // 2
N_SEL = PEER_HEADS * PEER_TOPK
SEL_ROWS = SUBLANES * N_SEL
U_LOW_ROWS, U_HIGH_ROWS = (7, 5, 3, 1), (6, 4, 2, 0)
V_LOW_ROWS, V_HIGH_ROWS = (0, 2, 4, 6), (1, 3, 5, 7)


def _pack_table(tab, low_rows, high_rows):
    e = tab.shape[0]
    bits = lax.bitcast_convert_type(tab.astype(BF16), jnp.uint16).astype(jnp.uint32).reshape(e, SUBLANES, LANES)
    low = jnp.stack([bits[:, i] for i in low_rows], axis=1)
    high = jnp.stack([bits[:, i] for i in high_rows], axis=1)
    return lax.bitcast_convert_type(low | (high << 16), I32).reshape(e * HALF_SUB, LANES)


def _stage_rows(idx_ref, tab_ref, stage_ref, p):
    for k in range(N_SEL):
        off = pl.multiple_of(idx_ref.at[k][p], HALF_SUB)
        stage_ref[k * HALF_SUB:(k + 1) * HALF_SUB, :] = tab_ref[pl.ds(off, HALF_SUB), :]


N_SLOTS = 4


def _staged_token_loop(n_tok, stage, compute):
    for s in range(N_SLOTS):
        stage(s, s)

    def group(j, carry):
        p = N_SLOTS * j
        for s in range(N_SLOTS):
            compute(p + s, s)
            stage(jnp.minimum(p + s + N_SLOTS, n_tok - 1), s)
        return carry

    lax.fori_loop(0, n_tok // N_SLOTS, group, 0)


FOLD_LAG = SUBLANES


def _peer_u_body(idx_ref, h_ref, g_ref, tab_ref, w_ref, *scratch):
    stages, (y_ref, a_ref) = scratch[:N_SLOTS], scratch[N_SLOTS:]
    n_tok = h_ref.shape[0]

    def fold(t):
        pieces = []
        for v in range(SEL_ROWS // LANES):
            z = pltpu.roll(y_ref[t, :, v * LANES:(v + 1) * LANES], LANES - SUBLANES + 1, 1, stride=1, stride_axis=0)
            pieces.append(jnp.sum(z, axis=0, keepdims=True))
        a_ref[pl.ds(t, 1), :] = jnp.concatenate(pieces, axis=1)

    def compute(p, slot):
        fold(p)
        m = pltpu.bitcast(stages[slot][...], BF16)
        y_ref[p + FOLD_LAG] = lax.dot_general(h_ref[p].astype(BF16), m, _NT, preferred_element_type=F32)

    y_ref[0:FOLD_LAG] = jnp.zeros((FOLD_LAG, SUBLANES, SEL_ROWS), F32)
    _staged_token_loop(n_tok, lambda p, slot: _stage_rows(idx_ref, tab_ref, stages[slot], p), compute)
    for t in range(n_tok, n_tok + FOLD_LAG):
        fold(t)
    w_ref[...] = g_ref[...] * jax.nn.gelu(a_ref[FOLD_LAG:n_tok + FOLD_LAG, :])


def _peer_v_body(idx_ref, w_ref, tab_ref, o_ref, *stages):

    def compute(p, slot):
        m = pltpu.bitcast(stages[slot][...], BF16)
        wrow = w_ref[pl.ds(p, 1), :]
        pieces = []
        for v in range(SEL_ROWS // LANES):
            wb = jnp.broadcast_to(wrow[:, v * LANES:(v + 1) * LANES], (SUBLANES, LANES))
            pieces.append(pltpu.roll(wb, 0, 1, stride=1, stride_axis=0))
        o_ref[p] = jnp.dot(jnp.concatenate(pieces, axis=1).astype(BF16), m, preferred_element_type=F32)

    _staged_token_loop(o_ref.shape[0], lambda p, slot: _stage_rows(idx_ref, tab_ref, stages[slot], p), compute)


def _table_spec(tab):
    return pl.BlockSpec(tab.shape, lambda i: (0, 0), pipeline_mode=pl.Buffered(1))


def _peer_u(idx, h3, g_sparse, tab, n_tiles):
    tb = GATHER_TILE
    wide = pl.BlockSpec((tb, SEL_ROWS), lambda i: (i, 0))
    return pl.pallas_call(
        _peer_u_body, grid=(n_tiles,),
        in_specs=[pl.BlockSpec((N_SEL, tb), lambda i: (0, i), memory_space=pltpu.SMEM, pipeline_mode=pl.Buffered(1)),
                  pl.BlockSpec((tb, SUBLANES, LANES), lambda i: (i, 0, 0)), wide, _table_spec(tab)],
        out_specs=wide,
        out_shape=jax.ShapeDtypeStruct((n_tiles * tb, SEL_ROWS), F32),
        scratch_shapes=[pltpu.VMEM((HALF_SUB * N_SEL, LANES), I32)] * N_SLOTS
        + [pltpu.VMEM((tb + FOLD_LAG, SUBLANES, SEL_ROWS), F32), pltpu.VMEM((tb + FOLD_LAG, SEL_ROWS), F32)],
        compiler_params=_cparams(("arbitrary",), vmem_mb=56), name="peer_u",
    )(idx, h3, g_sparse, tab)


def _peer_v(idx, w_sparse, tab, n_tiles):
    tb = GATHER_TILE
    return pl.pallas_call(
        _peer_v_body, grid=(n_tiles,),
        in_specs=[pl.BlockSpec((N_SEL, tb), lambda i: (0, i), memory_space=pltpu.SMEM, pipeline_mode=pl.Buffered(1)),
                  pl.BlockSpec((tb, SEL_ROWS), lambda i: (i, 0)), _table_spec(tab)],
        out_specs=pl.BlockSpec((tb, SUBLANES, LANES), lambda i: (i, 0, 0)),
        out_shape=jax.ShapeDtypeStruct((n_tiles * tb, SUBLANES, LANES), F32),
        scratch_shapes=[pltpu.VMEM((HALF_SUB * N_SEL, LANES), I32)] * N_SLOTS,
        compiler_params=_cparams(("arbitrary",), vmem_mb=56), name="peer_v",
    )(idx, w_sparse, tab)


def _ln2_body(mod_ref, x_ref, f_ref, g2_ref, lg_ref, lb_ref, o_ref):
    o_ref[...] = _layer_norm(ALPHA * x_ref[...] + g2_ref[0] * f_ref[...], lg_ref[...], lb_ref[...])


def _ln2(x1, ffn, g2, lg, lb, mod_idx, n_tiles):
    d = D_MODEL
    xt = pl.BlockSpec((TOK_TILE, d), lambda i, m: (i, 0))
    vec = pl.BlockSpec((1, d), lambda i, m: (0, 0))
    grid_spec = pltpu.PrefetchScalarGridSpec(
        num_scalar_prefetch=1, grid=(n_tiles,),
        in_specs=[xt, xt, pl.BlockSpec((1, 1, d), lambda i, m: (m[i], 0, 0)), vec, vec],
        out_specs=xt,
    )
    return pl.pallas_call(
        _ln2_body, grid_spec=grid_spec,
        out_shape=jax.ShapeDtypeStruct((n_tiles * TOK_TILE, d), F32),
        compiler_params=_cparams(("arbitrary",)), name="ln2",
    )(mod_idx, x1, ffn, g2, lg, lb)


def _mod_index(B, T, L, tile, n_tiles):
    start = np.arange(n_tiles) * tile
    return jnp.asarray(np.where(start < B * T, start // T, B), I32)


def _pad_w_in(w_in):
    d = w_in.shape[0]
    na, gla, conv, sgu, gates = (w_in[:, 0:768], w_in[:, 768:1568], w_in[:, 1568:2080],
                                 w_in[:, 2080:2592], w_in[:, 2592:6688])
    z224 = jnp.zeros((d, 1024 - 800), w_in.dtype)
    z256 = jnp.zeros((d, 1024 - 768), w_in.dtype)
    return jnp.concatenate([gates, conv, sgu, gla, z224, na, z256], axis=1).astype(BF16)


def kernel(x, c, ctx, c_ctx, ada_w, ada_b, w_in, na_rpb, gla_gate_up, gla_gate_b, gla_norm_g, conv_dw, conv_b,
           conv_ln_g, conv_ln_b, sgu_ln_g, sgu_ln_b, sgu_ws, sgu_bs, w_branch, w_out, ln1_g, ln1_b, peer_wq,
           peer_keys, peer_u, peer_v, ln2_g, ln2_b):
    B, T, D = x.shape
    L = ctx.shape[1]
    depth = ada_w.shape[0]
    n_lat, n_ctx = B * T, B * L
    ntok = n_lat + n_ctx
    kw, vw = GLA_HEADS * GLA_DK, GLA_HEADS * GLA_DV

    xa = jnp.concatenate([x.reshape(n_lat, D), ctx.reshape(n_ctx, D)], axis=0)
    cos_l, sin_l = _rope_tables(T)
    cos_c, sin_c = jnp.ones((L, kw), F32), jnp.zeros((L, kw), F32)
    s_zero = jnp.zeros((B, kw, vw), F32)
    n_mod = -(-(B + 1) // SUBLANES) * SUBLANES
    cpad = jnp.concatenate([c, c_ctx[None, :], jnp.zeros((n_mod - B - 1, D), F32)], axis=0)

    for l in range(depth):
        need_ctx = l < depth - 1
        n_act = ntok if need_ctx else n_lat
        mod = _ada(cpad, ada_w[l], ada_b[l][None, :])
        sh1, sc1, g1, sh2, sc2, g2 = [m[:, None, :] for m in jnp.split(mod, 6, axis=-1)]

        z = _modmm(xa, sc1, sh1, _pad_w_in(w_in[l]), _mod_index(B, T, L, TOK_TILE, ntok // TOK_TILE),
                   ntok // TOK_TILE, 1024)

        y_na = _na_attention(z, _na_bias_table(na_rpb[l]), B, T, L)

        gups = []
        for d in range(2):
            gu = jnp.zeros((LANES, kw), F32).at[d * GLA_GATE_RANK:(d + 1) * GLA_GATE_RANK].set(gla_gate_up[l, d])
            gups.append(gu.astype(BF16))
        gbs = [gla_gate_b[l, d][None, :] for d in range(2)]
        ctx_blocks = n_lat // L
        oc_f, sc_f = _gla_scan(z, cos_c, sin_c, gups[0], gbs[0], s_zero, B=B, seq=L, row0_blocks=ctx_blocks,
                               tile=L, reverse=False)
        oc_b, sc_b = _gla_scan(z, cos_c, sin_c, gups[1], gbs[1], s_zero, B=B, seq=L, row0_blocks=ctx_blocks,
                               tile=L, reverse=True)
        o_f, _ = _gla_scan(z, cos_l, sin_l, gups[0], gbs[0], sc_f, B=B, seq=T, row0_blocks=0, tile=TOK_TILE,
                           reverse=False)
        o_b, _ = _gla_scan(z, cos_l, sin_l, gups[1], gbs[1], sc_b, B=B, seq=T, row0_blocks=0, tile=TOK_TILE,
                           reverse=True)

        cv_args = (conv_dw[l], conv_b[l][None, :], conv_ln_g[l][None, :], conv_ln_b[l][None, :])
        y_cv = _conformer_conv(z, *cv_args, B=B, seq=T, row0_blocks=0)
        bs_exp = jnp.repeat(sgu_bs[l].T, BRANCH_W // SGU_GROUPS, axis=1)
        y_sg = _spatial_gating(z, sgu_ln_g[l][None, :], sgu_ln_b[l][None, :], sgu_ws[l].astype(BF16), bs_exp,
                               n_act // TOK_TILE)

        if need_ctx:
            y_na = jnp.concatenate([y_na, _ctx_attention(z, B, T, L)], axis=0)
            o_f = jnp.concatenate([o_f, oc_f], axis=0)
            o_b = jnp.concatenate([o_b, oc_b], axis=0)
            y_cv = jnp.concatenate([y_cv, _conformer_conv(z, *cv_args, B=B, seq=L, row0_blocks=ctx_blocks)], axis=0)

        x1, h2 = _merge(z, y_na, o_f, o_b, y_cv, y_sg, xa, g1, sc2, sh2, gla_norm_g[l].reshape(1, vw),
                        w_branch[l].astype(BF16), w_out[l].astype(BF16), ln1_g[l][None, :], ln1_b[l][None, :],
                        _mod_index(B, T, L, MERGE_TILE, n_act // MERGE_TILE), n_act // MERGE_TILE)

        q = _mm(h2, peer_wq[l].astype(BF16), n_act // TOK_TILE, 1024)
        idx_t, g_t = _peer_topk(q, peer_keys[l].astype(BF16), n_act // TOPK_TILE)
        idx = idx_t.reshape(N_SEL, n_act) * HALF_SUB
        gate = g_t.reshape(N_SEL, n_act).T
        g_sparse = jnp.pad(gate[:, :, None], ((0, 0), (0, 0), (0, SUBLANES - 1))).reshape(n_act, SEL_ROWS)
        w_sparse = _peer_u(idx, h2.reshape(n_act, SUBLANES, LANES), g_sparse,
                           _pack_table(peer_u[l], U_LOW_ROWS, U_HIGH_ROWS), n_act // GATHER_TILE)
        ffn = _peer_v(idx, w_sparse, _pack_table(peer_v[l], V_LOW_ROWS, V_HIGH_ROWS),
                      n_act // GATHER_TILE).reshape(n_act, D)

        xa = _ln2(x1, ffn, g2, ln2_g[l][None, :], ln2_b[l][None, :],
                  _mod_index(B, T, L, TOK_TILE, n_act // TOK_TILE), n_act // TOK_TILE)

    return xa[:n_lat].reshape(B, T, D)
```

```python
import functools

import numpy as np
import jax
import jax.numpy as jnp
from jax import lax
from jax.experimental import pallas as pl
from jax.experimental.pallas import tpu as pltpu

F32 = jnp.float32
BF16 = jnp.bfloat16
I32 = jnp.int32

D_MODEL = 1024
GRID_W = 64
BRANCH_W = 256
NA_HEADS = 4
NA_HEAD_DIM = 64
NA_WIN_H = 8
NA_WIN_W = 16
GLA_HEADS = 4
GLA_DV = 64
GLA_DK = 32
GLA_GATE_RANK = 16
GLA_TAU = 16.0
GLA_CHUNK = 64
ROPE_BASE = 100.0
CONV_WIDTH = 31
SGU_GROUPS = 4
SGU_CHUNK = 128
PEER_HEADS = 8
PEER_NKEYS = 128
PEER_DQ = 256
PEER_TOPK = 16
DEPTH = 2
ALPHA = (2 * DEPTH) ** 0.25
NEG_INF = -1e30
LN_EPS = 1e-6

SUBLANES = 8
LANES = 128

Z_GATES = 0
Z_CONV = 4096
Z_SGU = 4608
Z_GLA = 5120
Z_NA = 6144
Z_COLS = 7168

TOK_TILE = 512
GATHER_TILE = 128

_NT = (((1,), (1,)), ((), ()))
_TN = (((0,), (0,)), ((), ()))


def _cparams(sem, vmem_mb=48):
    return pltpu.CompilerParams(dimension_semantics=sem, vmem_limit_bytes=vmem_mb * 1024 * 1024)


def _split3(x):
    hi = x.astype(BF16)
    r1 = x - hi.astype(F32)
    mid = r1.astype(BF16)
    lo = (r1 - mid.astype(F32)).astype(BF16)
    return hi, mid, lo


def _dot_exact(a_bf16, x, dims=None, lhs_is_x=False):
    out = None
    for p in _split3(x):
        if dims is None:
            t = jnp.dot(p, a_bf16, preferred_element_type=F32) if lhs_is_x else jnp.dot(a_bf16, p, preferred_element_type=F32)
        else:
            t = lax.dot_general(p, a_bf16, dims, preferred_element_type=F32) if lhs_is_x else lax.dot_general(a_bf16, p, dims, preferred_element_type=F32)
        out = t if out is None else out + t
    return out


def _layer_norm(x, g, b):
    mu = jnp.mean(x, axis=-1, keepdims=True)
    xc = x - mu
    var = jnp.mean(xc * xc, axis=-1, keepdims=True)
    return xc * lax.rsqrt(var + LN_EPS) * g + b


ADA_TILE = 512


def _ada_body(c_ref, w_ref, b_ref, o_ref):
    c = c_ref[...]
    cs = c * jax.nn.sigmoid(c)
    o_ref[...] = jnp.dot(cs, w_ref[...], preferred_element_type=F32, precision=lax.Precision.HIGHEST) + b_ref[...]


def _ada(cpad, w, b):
    rows, d = cpad.shape
    n = w.shape[1]
    return pl.pallas_call(
        _ada_body, grid=(n // ADA_TILE,),
        in_specs=[pl.BlockSpec((rows, d), lambda j: (0, 0)),
                  pl.BlockSpec((d, ADA_TILE), lambda j: (0, j)),
                  pl.BlockSpec((1, ADA_TILE), lambda j: (0, j))],
        out_specs=pl.BlockSpec((rows, ADA_TILE), lambda j: (0, j)),
        out_shape=jax.ShapeDtypeStruct((rows, n), F32),
        compiler_params=_cparams(("arbitrary",)), name="ada",
    )(cpad, w, b)


def _modmm_body(mod_ref, x_ref, sc_ref, sh_ref, w_ref, o_ref, *, tn):
    h = (x_ref[...] * (1.0 + sc_ref[0]) + sh_ref[0]).astype(BF16)
    for j in range(w_ref.shape[1] // tn):
        o_ref[:, j * tn:(j + 1) * tn] = jnp.dot(h, w_ref[:, j * tn:(j + 1) * tn], preferred_element_type=F32)


def _mm_body(x_ref, w_ref, o_ref):
    o_ref[...] = jnp.dot(x_ref[...].astype(BF16), w_ref[...], preferred_element_type=F32)


def _modmm(x, sc, sh, w, mod_idx, n_tiles, tn):
    d = x.shape[1]
    n = w.shape[1]
    grid_spec = pltpu.PrefetchScalarGridSpec(
        num_scalar_prefetch=1,
        grid=(n_tiles,),
        in_specs=[
            pl.BlockSpec((TOK_TILE, d), lambda i, m: (i, 0)),
            pl.BlockSpec((1, 1, d), lambda i, m: (m[i], 0, 0)),
            pl.BlockSpec((1, 1, d), lambda i, m: (m[i], 0, 0)),
            pl.BlockSpec((d, n), lambda i, m: (0, 0), pipeline_mode=pl.Buffered(1)),
        ],
        out_specs=pl.BlockSpec((TOK_TILE, n), lambda i, m: (i, 0)),
    )
    return pl.pallas_call(
        functools.partial(_modmm_body, tn=tn), grid_spec=grid_spec,
        out_shape=jax.ShapeDtypeStruct((n_tiles * TOK_TILE, n), F32),
        compiler_params=_cparams(("arbitrary",), vmem_mb=56), name="modmm",
    )(mod_idx, x, sc, sh, w)


def _mm(x, w, n_tiles, tn):
    d = x.shape[1]
    n = w.shape[1]
    return pl.pallas_call(
        _mm_body, grid=(n // tn, n_tiles),
        in_specs=[pl.BlockSpec((TOK_TILE, d), lambda j, i: (i, 0)),
                  pl.BlockSpec((d, tn), lambda j, i: (0, j))],
        out_specs=pl.BlockSpec((TOK_TILE, tn), lambda j, i: (i, j)),
        out_shape=jax.ShapeDtypeStruct((n_tiles * TOK_TILE, n), F32),
        compiler_params=_cparams(("arbitrary", "arbitrary")), name="peer_q",
    )(x, w)


def _head_masks(width, per_head, heads):
    lane = lax.broadcasted_iota(I32, (1, width), 1)
    return [(lane >= h * per_head) & (lane < (h + 1) * per_head) for h in range(heads)]


def _na_body(q_ref, k_ref, v_ref, kc_ref, vc_ref, bias_ref, o_ref, *, rows):
    r = pl.program_id(1)
    r0 = jnp.clip(r - NA_WIN_H // 2, 0, rows - NA_WIN_H)
    delta = r - r0
    start = pl.multiple_of(r0 * GRID_W, GRID_W)
    nwin = NA_WIN_H * GRID_W
    kwin = k_ref[pl.ds(start, nwin), :].astype(BF16)
    vwin = v_ref[pl.ds(start, nwin), :].astype(BF16)
    kc = kc_ref[...].astype(BF16)
    vc = vc_ref[...].astype(BF16)
    q = q_ref[...] * (NA_HEAD_DIM ** -0.5)
    out = jnp.zeros(q.shape, F32)
    for h, hm in enumerate(_head_masks(BRANCH_W, NA_HEAD_DIM, NA_HEADS)):
        qh = jnp.where(hm, q, 0.0).astype(BF16)
        s = lax.dot_general(qh, kwin, _NT, preferred_element_type=F32) + bias_ref[h, delta]
        sc = lax.dot_general(qh, kc, _NT, preferred_element_type=F32)
        m = jnp.maximum(jnp.max(s, axis=1, keepdims=True), jnp.max(sc, axis=1, keepdims=True))
        e = jnp.exp(s - m)
        ec = jnp.exp(sc - m)
        den = jnp.sum(e, axis=1, keepdims=True) + jnp.sum(ec, axis=1, keepdims=True)
        oh = (jnp.dot(e.astype(BF16), vwin, preferred_element_type=F32)
              + jnp.dot(ec.astype(BF16), vc, preferred_element_type=F32))
        out = out + jnp.where(hm, oh / den, 0.0)
    o_ref[...] = out


def _na_bias_table(rpb):
    colv = np.arange(GRID_W)
    c0 = np.clip(colv - NA_WIN_W // 2, 0, GRID_W - NA_WIN_W)
    in_win = (colv[None, :] >= c0[:, None]) & (colv[None, :] < c0[:, None] + NA_WIN_W)
    edge = GRID_W - NA_WIN_W
    padded = jnp.pad(rpb, ((0, 0), (0, 0), (edge, edge)), mode="edge")
    cols = jnp.stack([padded[:, :, GRID_W - 1 - q:2 * GRID_W - 1 - q] for q in range(GRID_W)], axis=2)
    cols = jnp.where(jnp.asarray(in_win)[None, None], cols, NEG_INF)
    b = jnp.stack([cols[:, NA_WIN_H - 1 - d:2 * NA_WIN_H - 1 - d] for d in range(NA_WIN_H)], axis=1)
    b = b.transpose(0, 1, 3, 2, 4)
    return b.reshape(NA_HEADS, NA_WIN_H, GRID_W, NA_WIN_H * GRID_W).astype(F32)


def _na_attention(z, bias, B, T, L):
    rows = T // GRID_W
    cq, ck, cv = Z_NA // BRANCH_W, Z_NA // BRANCH_W + 1, Z_NA // BRANCH_W + 2
    ctx0 = (B * T) // L
    return pl.pallas_call(
        functools.partial(_na_body, rows=rows), grid=(B, rows),
        in_specs=[
            pl.BlockSpec((GRID_W, BRANCH_W), lambda b, r: (b * rows + r, cq)),
            pl.BlockSpec((T, BRANCH_W), lambda b, r: (b, ck)),
            pl.BlockSpec((T, BRANCH_W), lambda b, r: (b, cv)),
            pl.BlockSpec((L, BRANCH_W), lambda b, r: (ctx0 + b, ck)),
            pl.BlockSpec((L, BRANCH_W), lambda b, r: (ctx0 + b, cv)),
            pl.BlockSpec(bias.shape, lambda b, r: (0, 0, 0, 0)),
        ],
        out_specs=pl.BlockSpec((GRID_W, BRANCH_W), lambda b, r: (b * rows + r, 0)),
        out_shape=jax.ShapeDtypeStruct((B * T, BRANCH_W), F32),
        compiler_params=_cparams(("arbitrary", "arbitrary")), name="na_attn",
    )(z, z, z, z, z, bias)


def _ctx_attn_body(q_ref, k_ref, v_ref, o_ref):
    k = k_ref[...].astype(BF16)
    v = v_ref[...].astype(BF16)
    q = q_ref[...] * (NA_HEAD_DIM ** -0.5)
    out = jnp.zeros(q.shape, F32)
    for hm in _head_masks(BRANCH_W, NA_HEAD_DIM, NA_HEADS):
        qh = jnp.where(hm, q, 0.0).astype(BF16)
        s = lax.dot_general(qh, k, _NT, preferred_element_type=F32)
        m = jnp.max(s, axis=1, keepdims=True)
        e = jnp.exp(s - m)
        den = jnp.sum(e, axis=1, keepdims=True)
        oh = jnp.dot(e.astype(BF16), v, preferred_element_type=F32)
        out = out + jnp.where(hm, oh / den, 0.0)
    o_ref[...] = out


def _ctx_attention(z, B, T, L):
    cq = Z_NA // BRANCH_W
    ctx0 = (B * T) // L
    return pl.pallas_call(
        _ctx_attn_body, grid=(B,),
        in_specs=[pl.BlockSpec((L, BRANCH_W), lambda b, c=c: (ctx0 + b, cq + c)) for c in range(3)],
        out_specs=pl.BlockSpec((L, BRANCH_W), lambda b: (b, 0)),
        out_shape=jax.ShapeDtypeStruct((B * L, BRANCH_W), F32),
        compiler_params=_cparams(("arbitrary",)), name="ctx_attn",
    )(z, z, z)


def _gla_body(z_ref, cos_ref, sin_ref, gup_ref, gb_ref, s0_ref, o_ref, sfin_ref, state_ref, *, reverse, n_steps):
    g = pl.program_id(1)

    @pl.when(g == 0)
    def _():
        state_ref[...] = s0_ref[0]

    kw = GLA_HEADS * GLA_DK
    vw = GLA_HEADS * GLA_DV
    q = z_ref[:, 0:kw] * (GLA_DK ** -0.5)
    k = z_ref[:, kw:2 * kw]
    v = z_ref[:, 2 * kw:2 * kw + vw]
    lo = z_ref[:, 2 * kw + 2 * vw:2 * kw + 2 * vw + LANES]
    cos = cos_ref[...]
    sin = sin_ref[...]
    lane = lax.broadcasted_iota(I32, (1, kw), 1)
    first = (lane % (GLA_DK // 2)) < (GLA_DK // 4)

    def rope(x):
        partner = jnp.where(first, pltpu.roll(x, kw - GLA_DK // 4, 1), pltpu.roll(x, GLA_DK // 4, 1))
        return x * cos + partner * sin

    q = rope(q)
    k = rope(k)
    logits = jnp.dot(lo.astype(BF16), gup_ref[...], preferred_element_type=F32) + gb_ref[...]
    la = (jnp.minimum(logits, 0.0) - jnp.log1p(jnp.exp(-jnp.abs(logits)))) / GLA_TAU

    C = GLA_CHUNK
    ri = lax.broadcasted_iota(I32, (C, C), 0)
    ci = lax.broadcasted_iota(I32, (C, C), 1)
    tri = (ri <= ci) if reverse else (ri >= ci)
    cum = jnp.where(tri, 1.0, 0.0).astype(BF16)
    tri4 = jnp.concatenate([tri] * GLA_HEADS, axis=0)
    hm_k = _head_masks(kw, GLA_DK, GLA_HEADS)
    hm_v = _head_masks(vw, GLA_DV, GLA_HEADS)
    srow = lax.broadcasted_iota(I32, (kw, vw), 0) // GLA_DK
    scol = lax.broadcasted_iota(I32, (kw, vw), 1) // GLA_DV
    blockmask = srow == scol
    ones_cv = jnp.ones((C, vw), BF16)

    S = state_ref[...]
    n_chunks = z_ref.shape[0] // C
    order = range(n_chunks - 1, -1, -1) if reverse else range(n_chunks)
    for c in order:
        sl = slice(c * C, (c + 1) * C)
        la_c = la[sl]
        b = _dot_exact(cum, la_c)
        bl = b[0:1] if reverse else b[C - 1:C]
        qs = q[sl] * jnp.exp(b)
        ks = k[sl] * jnp.exp(-b)
        ke = k[sl] * jnp.exp(bl - b)
        vb = v[sl].astype(BF16)
        qs_b = qs.astype(BF16)
        qstack = jnp.concatenate([jnp.where(hm, qs, 0.0) for hm in hm_k], axis=0).astype(BF16)
        a = lax.dot_general(qstack, ks.astype(BF16), _NT, preferred_element_type=F32)
        a = jnp.where(tri4, a, 0.0)
        o_stack = jnp.dot(a.astype(BF16), vb, preferred_element_type=F32)
        o_c = jnp.dot(qs_b, S.astype(BF16), preferred_element_type=F32)
        for h, hm in enumerate(hm_v):
            o_c = o_c + jnp.where(hm, o_stack[h * C:(h + 1) * C], 0.0)
        o_ref[sl, :] = o_c
        u = lax.dot_general(ke.astype(BF16), vb, _TN, preferred_element_type=F32)
        dcol = _dot_exact(ones_cv, la_c, dims=_TN, lhs_is_x=True)
        S = jnp.exp(dcol) * S + jnp.where(blockmask, u, 0.0)
    state_ref[...] = S

    @pl.when(g == n_steps - 1)
    def _():
        sfin_ref[0] = S


def _gla_scan(z, cos, sin, gup, gb, s0, *, B, seq, row0_blocks, tile, reverse):
    n_steps = seq // tile
    kw = GLA_HEADS * GLA_DK
    vw = GLA_HEADS * GLA_DV

    def step(g):
        return n_steps - 1 - g if reverse else g

    return pl.pallas_call(
        functools.partial(_gla_body, reverse=reverse, n_steps=n_steps), grid=(B, n_steps),
        in_specs=[
            pl.BlockSpec((tile, 1024), lambda b, g: (row0_blocks + b * n_steps + step(g), Z_GLA // 1024)),
            pl.BlockSpec((tile, kw), lambda b, g: (step(g), 0)),
            pl.BlockSpec((tile, kw), lambda b, g: (step(g), 0)),
            pl.BlockSpec((LANES, kw), lambda b, g: (0, 0)),
            pl.BlockSpec((1, kw), lambda b, g: (0, 0)),
            pl.BlockSpec((1, kw, vw), lambda b, g: (b, 0, 0)),
        ],
        out_specs=[
            pl.BlockSpec((tile, vw), lambda b, g: (b * n_steps + step(g), 0)),
            pl.BlockSpec((1, kw, vw), lambda b, g: (b, 0, 0)),
        ],
        out_shape=[jax.ShapeDtypeStruct((B * seq, vw), F32), jax.ShapeDtypeStruct((B, kw, vw), F32)],
        scratch_shapes=[pltpu.VMEM((kw, vw), F32)],
        compiler_params=_cparams(("arbitrary", "arbitrary")), name="gla_rev" if reverse else "gla_fwd",
    )(z, cos, sin, gup, gb, s0)


def _rope_tables(T):
    t = np.arange(T)
    row, col = t // GRID_W, t % GRID_W
    nf = GLA_DK // 4
    inv = 1.0 / (ROPE_BASE ** (jnp.arange(nf, dtype=F32) / nf))
    j = np.arange(GLA_HEADS * GLA_DK)
    d = j % GLA_DK
    use_col = (d // (GLA_DK // 2)) == 1
    e = d % (GLA_DK // 2)
    fi = e % nf
    pos = jnp.where(jnp.asarray(use_col)[None, :], jnp.asarray(col, F32)[:, None], jnp.asarray(row, F32)[:, None])
    ang = pos * inv[fi][None, :]
    sign = jnp.asarray(np.where(e < nf, -1.0, 1.0), F32)[None, :]
    return jnp.cos(ang), jnp.sin(ang) * sign


CONV_PAD = 16
CONV_ROWS = 128


def _conv_body(z_ref, dw_ref, b_ref, g_ref, be_ref, o_ref, ypad_ref, *, seq):
    zeros = jnp.zeros((CONV_PAD, BRANCH_W), F32)
    ypad_ref[0:CONV_PAD, :] = zeros
    ypad_ref[seq + CONV_PAD:seq + 2 * CONV_PAD, :] = zeros

    def glu(i, carry):
        base = pl.multiple_of(i * CONV_ROWS, CONV_ROWS)
        a = z_ref[pl.ds(base, CONV_ROWS), 0:BRANCH_W]
        gate = z_ref[pl.ds(base, CONV_ROWS), BRANCH_W:2 * BRANCH_W]
        ypad_ref[pl.ds(base + CONV_PAD, CONV_ROWS), :] = a * jax.nn.sigmoid(gate)
        return carry

    lax.fori_loop(0, seq // CONV_ROWS, glu, 0)

    def tile(i, carry):
        base = pl.multiple_of(i * CONV_ROWS, CONV_ROWS)
        acc = jnp.zeros((CONV_ROWS, BRANCH_W), F32)
        win = ypad_ref[pl.ds(base, CONV_ROWS + 2 * CONV_PAD), :]
        for j in range(CONV_WIDTH):
            off = CONV_PAD - CONV_WIDTH // 2 + j
            acc = acc + win[off:off + CONV_ROWS] * dw_ref[j:j + 1, :]
        y = _layer_norm(acc + b_ref[...], g_ref[...], be_ref[...])
        o_ref[pl.ds(base, CONV_ROWS), :] = y * jax.nn.sigmoid(y)
        return carry

    lax.fori_loop(0, seq // CONV_ROWS, tile, 0)


def _conformer_conv(z, dw, b, g, be, *, B, seq, row0_blocks):
    vec = pl.BlockSpec((1, BRANCH_W), lambda i: (0, 0))
    return pl.pallas_call(
        functools.partial(_conv_body, seq=seq), grid=(B,),
        in_specs=[pl.BlockSpec((seq, 2 * BRANCH_W), lambda i: (row0_blocks + i, Z_CONV // (2 * BRANCH_W))),
                  pl.BlockSpec((CONV_WIDTH, BRANCH_W), lambda i: (0, 0)), vec, vec, vec],
        out_specs=pl.BlockSpec((seq, BRANCH_W), lambda i: (i, 0)),
        out_shape=jax.ShapeDtypeStruct((B * seq, BRANCH_W), F32),
        scratch_shapes=[pltpu.VMEM((seq + 2 * CONV_PAD, BRANCH_W), F32)],
        compiler_params=_cparams(("arbitrary",)), name="conformer_conv",
    )(z, dw, b, g, be)


def _sgu_body(z_ref, g_ref, b_ref, ws_ref, bs_ref, o_ref):
    zz = jax.nn.gelu(z_ref[...])
    u = zz[:, 0:BRANCH_W]
    v = _layer_norm(zz[:, BRANCH_W:2 * BRANCH_W], g_ref[...], b_ref[...])
    gms = _head_masks(BRANCH_W, BRANCH_W // SGU_GROUPS, SGU_GROUPS)
    for c in range(z_ref.shape[0] // SGU_CHUNK):
        sl = slice(c * SGU_CHUNK, (c + 1) * SGU_CHUNK)
        vb = v[sl].astype(BF16)
        s = bs_ref[...]
        for gi, gm in enumerate(gms):
            s = s + jnp.where(gm, jnp.dot(ws_ref[gi], vb, preferred_element_type=F32), 0.0)
        o_ref[sl, :] = u[sl] * s


def _spatial_gating(z, g, b, ws, bs_exp, n_tiles):
    vec = pl.BlockSpec((1, BRANCH_W), lambda i: (0, 0))
    return pl.pallas_call(
        _sgu_body, grid=(n_tiles,),
        in_specs=[pl.BlockSpec((TOK_TILE, 2 * BRANCH_W), lambda i: (i, Z_SGU // (2 * BRANCH_W))), vec, vec,
                  pl.BlockSpec(ws.shape, lambda i: (0, 0, 0)),
                  pl.BlockSpec((SGU_CHUNK, BRANCH_W), lambda i: (0, 0))],
        out_specs=pl.BlockSpec((TOK_TILE, BRANCH_W), lambda i: (i, 0)),
        out_shape=jax.ShapeDtypeStruct((n_tiles * TOK_TILE, BRANCH_W), F32),
        compiler_params=_cparams(("arbitrary",)), name="sgu",
    )(z, g, b, ws, bs_exp)


MERGE_TILE = 256


def _merge_body(mod_ref, gates_ref, yna_ref, of_ref, ob_ref, r_ref, ycv_ref, ysg_ref, x_ref, g1_ref, sc2_ref, sh2_ref,
                ng_ref, wb_ref, wo_ref, lg_ref, lb_ref, x1_ref, h2_ref):
    o = of_ref[...] + ob_ref[...]
    vw = GLA_HEADS * GLA_DV
    hr = lax.broadcasted_iota(I32, (vw, vw), 0) // GLA_DV
    hc = lax.broadcasted_iota(I32, (vw, vw), 1) // GLA_DV
    same_head = jnp.where(hr == hc, 1.0, 0.0).astype(BF16)
    ms = _dot_exact(same_head, o * o, lhs_is_x=True) * (1.0 / GLA_DV)
    r = r_ref[...]
    y_gla = o * lax.rsqrt(ms + LN_EPS) * ng_ref[...] * (r * jax.nn.sigmoid(r))
    ys = (yna_ref[...], y_gla, ycv_ref[...], ysg_ref[...])
    merged = None
    for i in range(4):
        gate = jax.nn.sigmoid(gates_ref[:, i * D_MODEL:(i + 1) * D_MODEL])
        term = gate * jnp.dot(ys[i].astype(BF16), wb_ref[i], preferred_element_type=F32)
        merged = term if merged is None else merged + term
    y = jnp.dot(merged.astype(BF16), wo_ref[...], preferred_element_type=F32)
    x1 = _layer_norm(ALPHA * x_ref[...] + g1_ref[0] * y, lg_ref[...], lb_ref[...])
    x1_ref[...] = x1
    h2_ref[...] = x1 * (1.0 + sc2_ref[0]) + sh2_ref[0]


def _merge(z, yna, of, ob, ycv, ysg, x, g1, sc2, sh2, ng, wb, wo, lg, lb, mod_idx, n_tiles):
    d = D_MODEL
    br = pl.BlockSpec((MERGE_TILE, BRANCH_W), lambda i, m: (i, 0))
    modv = pl.BlockSpec((1, 1, d), lambda i, m: (m[i], 0, 0))
    vec = pl.BlockSpec((1, d), lambda i, m: (0, 0))
    xt = pl.BlockSpec((MERGE_TILE, d), lambda i, m: (i, 0))
    grid_spec = pltpu.PrefetchScalarGridSpec(
        num_scalar_prefetch=1, grid=(n_tiles,),
        in_specs=[
            pl.BlockSpec((MERGE_TILE, 4 * d), lambda i, m: (i, 0)),
            br, br, br,
            pl.BlockSpec((MERGE_TILE, BRANCH_W), lambda i, m: (i, (Z_GLA + 512) // BRANCH_W)),
            br, br, xt, modv, modv, modv,
            pl.BlockSpec((1, BRANCH_W), lambda i, m: (0, 0)),
            pl.BlockSpec(wb.shape, lambda i, m: (0, 0, 0)),
            pl.BlockSpec(wo.shape, lambda i, m: (0, 0)),
            vec, vec,
        ],
        out_specs=[xt, xt],
    )
    return pl.pallas_call(
        _merge_body, grid_spec=grid_spec,
        out_shape=[jax.ShapeDtypeStruct((n_tiles * MERGE_TILE, d), F32)] * 2,
        compiler_params=_cparams(("arbitrary",)), name="merge",
    )(mod_idx, z, yna, of, ob, z, ycv, ysg, x, g1, sc2, sh2, ng, wb, wo, lg, lb)


TOPK_TILE = 256


def _top16(vals, payload=None):
    n_cand = vals.shape[0]
    pos_iota = lax.broadcasted_iota(I32, vals.shape, 0).astype(F32)
    out_v, out_p = [], []
    for _ in range(PEER_TOPK):
        m = jnp.max(vals, axis=0, keepdims=True)
        pos = jnp.min(jnp.where(vals == m, pos_iota, float(n_cand)), axis=0, keepdims=True)
        hit = pos_iota == pos
        out_v.append(m)
        out_p.append(pos if payload is None else jnp.max(jnp.where(hit, payload, -1.0), axis=0, keepdims=True))
        vals = jnp.where(hit, -jnp.inf, vals)
    return jnp.concatenate(out_v, axis=0), jnp.concatenate(out_p, axis=0)


_PAIR_COUNTS = [PEER_TOPK // (i + 1) for i in range(PEER_TOPK)]


TOPK_HEADS = 2


def _topk_body(q_ref, keys_ref, row_ref, g_ref):
    half = PEER_DQ // 2
    n_tok = q_ref.shape[0]
    for h in range(TOPK_HEADS):
        tops = []
        for s in range(2):
            qs = q_ref[:, (2 * h + s) * half:(2 * h + s + 1) * half].astype(BF16)
            sc = lax.dot_general(keys_ref[h, s], qs, _NT, preferred_element_type=F32)
            tops.append(_top16(sc))
        (a, ia), (b, ib) = tops
        cand = [a[i:i + 1] + b[0:n] for i, n in enumerate(_PAIR_COUNTS)]
        code = [ia[i:i + 1] * float(PEER_NKEYS) + ib[0:n] for i, n in enumerate(_PAIR_COUNTS)]
        pad = -sum(_PAIR_COUNTS) % SUBLANES
        cand = jnp.concatenate(cand + [jnp.full((pad, n_tok), -jnp.inf, F32)], axis=0)
        code = jnp.concatenate(code + [jnp.zeros((pad, n_tok), F32)], axis=0)
        best, expert = _top16(cand, code)
        e = jnp.exp(best - jnp.max(best, axis=0, keepdims=True))
        g_ref[h] = e / jnp.sum(e, axis=0, keepdims=True)
        row_ref[h] = expert.astype(I32) * HALF_SUB


def _peer_topk(q, keys, n_tiles):
    ntok = n_tiles * TOPK_TILE
    out = pl.BlockSpec((TOPK_HEADS, PEER_TOPK, TOPK_TILE), lambda i, h: (h, 0, i))
    return pl.pallas_call(
        _topk_body, grid=(n_tiles, PEER_HEADS // TOPK_HEADS),
        in_specs=[pl.BlockSpec((TOPK_TILE, TOPK_HEADS * PEER_DQ), lambda i, h: (i, h)),
                  pl.BlockSpec((TOPK_HEADS, 2, PEER_NKEYS, PEER_DQ // 2), lambda i, h: (h, 0, 0, 0))],
        out_specs=[out, out],
        out_shape=[jax.ShapeDtypeStruct((PEER_HEADS, PEER_TOPK, ntok), I32),
                   jax.ShapeDtypeStruct((PEER_HEADS, PEER_TOPK, ntok), F32)],
        compiler_params=_cparams(("arbitrary", "arbitrary")), name="peer_topk",
    )(q, keys)


HALF_SUB = SUBLANES // 2
N_SEL = PEER_HEADS * PEER_TOPK
SEL_ROWS = SUBLANES * N_SEL
U_LOW_ROWS, U_HIGH_ROWS = (7, 5, 3, 1), (6, 4, 2, 0)
V_LOW_ROWS, V_HIGH_ROWS = (0, 2, 4, 6), (1, 3, 5, 7)


PACK_TILE = 512


def _pack_body(t_ref, o_ref, *, low_rows, high_rows):
    def bf16_bits(x):
        return pltpu.bitcast(x.astype(BF16).astype(F32), I32)

    for s in range(HALF_SUB):
        low = bf16_bits(t_ref[:, low_rows[s] * LANES:(low_rows[s] + 1) * LANES])
        high = bf16_bits(t_ref[:, high_rows[s] * LANES:(high_rows[s] + 1) * LANES])
        o_ref[:, s, :] = lax.shift_right_logical(low, 16) | high


def _pack_table(tab, low_rows, high_rows):
    e, d = tab.shape
    packed = pl.pallas_call(
        functools.partial(_pack_body, low_rows=low_rows, high_rows=high_rows), grid=(e // PACK_TILE,),
        in_specs=[pl.BlockSpec((PACK_TILE, d), lambda i: (i, 0))],
        out_specs=pl.BlockSpec((PACK_TILE, HALF_SUB, LANES), lambda i: (i, 0, 0)),
        out_shape=jax.ShapeDtypeStruct((e, HALF_SUB, LANES), I32),
        compiler_params=_cparams(("arbitrary",)), name="pack_table",
    )(tab)
    return packed.reshape(e * HALF_SUB, LANES)


def _stage_rows(idx_ref, tab_ref, stage_ref, p):
    for k in range(N_SEL):
        off = pl.multiple_of(idx_ref.at[k][p], HALF_SUB)
        stage_ref[k * HALF_SUB:(k + 1) * HALF_SUB, :] = tab_ref[pl.ds(off, HALF_SUB), :]


N_SLOTS = 4


def _staged_token_loop(n_tok, stage, compute):
    for s in range(N_SLOTS):
        stage(s, s)

    def group(j, carry):
        p = N_SLOTS * j
        for s in range(N_SLOTS):
            compute(p + s, s)
            stage(jnp.minimum(p + s + N_SLOTS, n_tok - 1), s)
        return carry

    lax.fori_loop(0, n_tok // N_SLOTS, group, 0)


FOLD_LAG = SUBLANES


def _peer_u_body(idx_ref, h_ref, g_ref, tab_ref, w_ref, *scratch):
    stages, (y_ref, a_ref) = scratch[:N_SLOTS], scratch[N_SLOTS:]
    n_tok = h_ref.shape[0]

    def fold(t):
        pieces = []
        for v in range(SEL_ROWS // LANES):
            z = pltpu.roll(y_ref[t, :, v * LANES:(v + 1) * LANES], LANES - SUBLANES + 1, 1, stride=1, stride_axis=0)
            pieces.append(jnp.sum(z, axis=0, keepdims=True))
        a_ref[pl.ds(t, 1), :] = jnp.concatenate(pieces, axis=1)

    def compute(p, slot):
        fold(p)
        m = pltpu.bitcast(stages[slot][...], BF16)
        y_ref[p + FOLD_LAG] = lax.dot_general(h_ref[p].astype(BF16), m, _NT, preferred_element_type=F32)

    y_ref[0:FOLD_LAG] = jnp.zeros((FOLD_LAG, SUBLANES, SEL_ROWS), F32)
    _staged_token_loop(n_tok, lambda p, slot: _stage_rows(idx_ref, tab_ref, stages[slot], p), compute)
    for t in range(n_tok, n_tok + FOLD_LAG):
        fold(t)
    w_ref[...] = g_ref[...] * jax.nn.gelu(a_ref[FOLD_LAG:n_tok + FOLD_LAG, :])


def _peer_v_body(idx_ref, w_ref, tab_ref, o_ref, *stages):

    def compute(p, slot):
        m = pltpu.bitcast(stages[slot][...], BF16)
        wrow = w_ref[pl.ds(p, 1), :]
        pieces = []
        for v in range(SEL_ROWS // LANES):
            wb = jnp.broadcast_to(wrow[:, v * LANES:(v + 1) * LANES], (SUBLANES, LANES))
            pieces.append(pltpu.roll(wb, 0, 1, stride=1, stride_axis=0))
        o_ref[p] = jnp.dot(jnp.concatenate(pieces, axis=1).astype(BF16), m, preferred_element_type=F32)

    _staged_token_loop(o_ref.shape[0], lambda p, slot: _stage_rows(idx_ref, tab_ref, stages[slot], p), compute)


def _table_spec(tab):
    return pl.BlockSpec(tab.shape, lambda i: (0, 0), pipeline_mode=pl.Buffered(1))


def _peer_u(idx, h3, g_sparse, tab, n_tiles):
    tb = GATHER_TILE
    wide = pl.BlockSpec((tb, SEL_ROWS), lambda i: (i, 0))
    return pl.pallas_call(
        _peer_u_body, grid=(n_tiles,),
        in_specs=[pl.BlockSpec((N_SEL, tb), lambda i: (0, i), memory_space=pltpu.SMEM, pipeline_mode=pl.Buffered(1)),
                  pl.BlockSpec((tb, SUBLANES, LANES), lambda i: (i, 0, 0)), wide, _table_spec(tab)],
        out_specs=wide,
        out_shape=jax.ShapeDtypeStruct((n_tiles * tb, SEL_ROWS), F32),
        scratch_shapes=[pltpu.VMEM((HALF_SUB * N_SEL, LANES), I32)] * N_SLOTS
        + [pltpu.VMEM((tb + FOLD_LAG, SUBLANES, SEL_ROWS), F32), pltpu.VMEM((tb + FOLD_LAG, SEL_ROWS), F32)],
        compiler_params=_cparams(("arbitrary",), vmem_mb=56), name="peer_u",
    )(idx, h3, g_sparse, tab)


def _peer_v(idx, w_sparse, tab, n_tiles):
    tb = GATHER_TILE
    return pl.pallas_call(
        _peer_v_body, grid=(n_tiles,),
        in_specs=[pl.BlockSpec((N_SEL, tb), lambda i: (0, i), memory_space=pltpu.SMEM, pipeline_mode=pl.Buffered(1)),
                  pl.BlockSpec((tb, SEL_ROWS), lambda i: (i, 0)), _table_spec(tab)],
        out_specs=pl.BlockSpec((tb, SUBLANES, LANES), lambda i: (i, 0, 0)),
        out_shape=jax.ShapeDtypeStruct((n_tiles * tb, SUBLANES, LANES), F32),
        scratch_shapes=[pltpu.VMEM((HALF_SUB * N_SEL, LANES), I32)] * N_SLOTS,
        compiler_params=_cparams(("arbitrary",), vmem_mb=56), name="peer_v",
    )(idx, w_sparse, tab)


def _ln2_body(mod_ref, x_ref, f_ref, g2_ref, lg_ref, lb_ref, o_ref):
    o_ref[...] = _layer_norm(ALPHA * x_ref[...] + g2_ref[0] * f_ref[...], lg_ref[...], lb_ref[...])


def _ln2(x1, ffn, g2, lg, lb, mod_idx, n_tiles):
    d = D_MODEL
    xt = pl.BlockSpec((TOK_TILE, d), lambda i, m: (i, 0))
    vec = pl.BlockSpec((1, d), lambda i, m: (0, 0))
    grid_spec = pltpu.PrefetchScalarGridSpec(
        num_scalar_prefetch=1, grid=(n_tiles,),
        in_specs=[xt, xt, pl.BlockSpec((1, 1, d), lambda i, m: (m[i], 0, 0)), vec, vec],
        out_specs=xt,
    )
    return pl.pallas_call(
        _ln2_body, grid_spec=grid_spec,
        out_shape=jax.ShapeDtypeStruct((n_tiles * TOK_TILE, d), F32),
        compiler_params=_cparams(("arbitrary",)), name="ln2",
    )(mod_idx, x1, ffn, g2, lg, lb)


def _mod_index(B, T, L, tile, n_tiles):
    start = np.arange(n_tiles) * tile
    return jnp.asarray(np.where(start < B * T, start // T, B), I32)


def _pad_w_in(w_in):
    d = w_in.shape[0]
    na, gla, conv, sgu, gates = (w_in[:, 0:768], w_in[:, 768:1568], w_in[:, 1568:2080],
                                 w_in[:, 2080:2592], w_in[:, 2592:6688])
    z224 = jnp.zeros((d, 1024 - 800), w_in.dtype)
    z256 = jnp.zeros((d, 1024 - 768), w_in.dtype)
    return jnp.concatenate([gates, conv, sgu, gla, z224, na, z256], axis=1).astype(BF16)


def kernel(x, c, ctx, c_ctx, ada_w, ada_b, w_in, na_rpb, gla_gate_up, gla_gate_b, gla_norm_g, conv_dw, conv_b,
           conv_ln_g, conv_ln_b, sgu_ln_g, sgu_ln_b, sgu_ws, sgu_bs, w_branch, w_out, ln1_g, ln1_b, peer_wq,
           peer_keys, peer_u, peer_v, ln2_g, ln2_b):
    B, T, D = x.shape
    L = ctx.shape[1]
    depth = ada_w.shape[0]
    n_lat, n_ctx = B * T, B * L
    ntok = n_lat + n_ctx
    kw, vw = GLA_HEADS * GLA_DK, GLA_HEADS * GLA_DV

    xa = jnp.concatenate([x.reshape(n_lat, D), ctx.reshape(n_ctx, D)], axis=0)
    cos_l, sin_l = _rope_tables(T)
    cos_c, sin_c = jnp.ones((L, kw), F32), jnp.zeros((L, kw), F32)
    s_zero = jnp.zeros((B, kw, vw), F32)
    n_mod = -(-(B + 1) // SUBLANES) * SUBLANES
    cpad = jnp.concatenate([c, c_ctx[None, :], jnp.zeros((n_mod - B - 1, D), F32)], axis=0)

    for l in range(depth):
        need_ctx = l < depth - 1
        n_act = ntok if need_ctx else n_lat
        mod = _ada(cpad, ada_w[l], ada_b[l][None, :])
        sh1, sc1, g1, sh2, sc2, g2 = [m[:, None, :] for m in jnp.split(mod, 6, axis=-1)]

        z = _modmm(xa, sc1, sh1, _pad_w_in(w_in[l]), _mod_index(B, T, L, TOK_TILE, ntok // TOK_TILE),
                   ntok // TOK_TILE, 1024)

        y_na = _na_attention(z, _na_bias_table(na_rpb[l]), B, T, L)

        gups = []
        for d in range(2):
            gu = jnp.zeros((LANES, kw), F32).at[d * GLA_GATE_RANK:(d + 1) * GLA_GATE_RANK].set(gla_gate_up[l, d])
            gups.append(gu.astype(BF16))
        gbs = [gla_gate_b[l, d][None, :] for d in range(2)]
        ctx_blocks = n_lat // L
        oc_f, sc_f = _gla_scan(z, cos_c, sin_c, gups[0], gbs[0], s_zero, B=B, seq=L, row0_blocks=ctx_blocks,
                               tile=L, reverse=False)
        oc_b, sc_b = _gla_scan(z, cos_c, sin_c, gups[1], gbs[1], s_zero, B=B, seq=L, row0_blocks=ctx_blocks,
                               tile=L, reverse=True)
        o_f, _ = _gla_scan(z, cos_l, sin_l, gups[0], gbs[0], sc_f, B=B, seq=T, row0_blocks=0, tile=TOK_TILE,
                           reverse=False)
        o_b, _ = _gla_scan(z, cos_l, sin_l, gups[1], gbs[1], sc_b, B=B, seq=T, row0_blocks=0, tile=TOK_TILE,
                           reverse=True)

        cv_args = (conv_dw[l], conv_b[l][None, :], conv_ln_g[l][None, :], conv_ln_b[l][None, :])
        y_cv = _conformer_conv(z, *cv_args, B=B, seq=T, row0_blocks=0)
        bs_exp = jnp.repeat(sgu_bs[l].T, BRANCH_W // SGU_GROUPS, axis=1)
        y_sg = _spatial_gating(z, sgu_ln_g[l][None, :], sgu_ln_b[l][None, :], sgu_ws[l].astype(BF16), bs_exp,
                               n_act // TOK_TILE)

        if need_ctx:
            y_na = jnp.concatenate([y_na, _ctx_attention(z, B, T, L)], axis=0)
            o_f = jnp.concatenate([o_f, oc_f], axis=0)
            o_b = jnp.concatenate([o_b, oc_b], axis=0)
            y_cv = jnp.concatenate([y_cv, _conformer_conv(z, *cv_args, B=B, seq=L, row0_blocks=ctx_blocks)], axis=0)

        x1, h2 = _merge(z, y_na, o_f, o_b, y_cv, y_sg, xa, g1, sc2, sh2, gla_norm_g[l].reshape(1, vw),
                        w_branch[l].astype(BF16), w_out[l].astype(BF16), ln1_g[l][None, :], ln1_b[l][None, :],
                        _mod_index(B, T, L, MERGE_TILE, n_act // MERGE_TILE), n_act // MERGE_TILE)

        q = _mm(h2, peer_wq[l].astype(BF16), n_act // TOK_TILE, 1024)
        idx_t, g_t = _peer_topk(q, peer_keys[l].astype(BF16), n_act // TOPK_TILE)
        idx = idx_t.reshape(N_SEL, n_act)
        gate = g_t.reshape(N_SEL, n_act).T
        g_sparse = jnp.pad(gate[:, :, None], ((0, 0), (0, 0), (0, SUBLANES - 1))).reshape(n_act, SEL_ROWS)
        w_sparse = _peer_u(idx, h2.reshape(n_act, SUBLANES, LANES), g_sparse,
                           _pack_table(peer_u[l], U_LOW_ROWS, U_HIGH_ROWS), n_act // GATHER_TILE)
        ffn = _peer_v(idx, w_sparse, _pack_table(peer_v[l], V_LOW_ROWS, V_HIGH_ROWS),
                      n_act // GATHER_TILE).reshape(n_act, D)

        xa = _ln2(x1, ffn, g2, ln2_g[l][None, :], ln2_b[l][None, :],
                  _mod_index(B, T, L, TOK_TILE, n_act // TOK_TILE), n_act // TOK_TILE)

    return xa[:n_lat].reshape(B, T, D)
```

```python
import functools

import numpy as np
import jax
import jax.numpy as jnp
from jax import lax
from jax.experimental import pallas as pl
from jax.experimental.pallas import tpu as pltpu

F32 = jnp.float32
BF16 = jnp.bfloat16
I32 = jnp.int32

D_MODEL = 1024
GRID_W = 64
BRANCH_W = 256
NA_HEADS = 4
NA_HEAD_DIM = 64
NA_WIN_H = 8
NA_WIN_W = 16
GLA_HEADS = 4
GLA_DV = 64
GLA_DK = 32
GLA_GATE_RANK = 16
GLA_TAU = 16.0
GLA_CHUNK = 64
ROPE_BASE = 100.0
CONV_WIDTH = 31
SGU_GROUPS = 4
SGU_CHUNK = 128
PEER_HEADS = 8
PEER_NKEYS = 128
PEER_DQ = 256
PEER_TOPK = 16
DEPTH = 2
ALPHA = (2 * DEPTH) ** 0.25
NEG_INF = -1e30
LN_EPS = 1e-6

SUBLANES = 8
LANES = 128

Z_GATES = 0
Z_CONV = 4096
Z_SGU = 4608
Z_GLA = 5120
Z_NA = 6144
Z_COLS = 7168

TOK_TILE = 512
GATHER_TILE = 128

_NT = (((1,), (1,)), ((), ()))
_TN = (((0,), (0,)), ((), ()))


def _cparams(sem, vmem_mb=48):
    return pltpu.CompilerParams(dimension_semantics=sem, vmem_limit_bytes=vmem_mb * 1024 * 1024)


def _split3(x):
    hi = x.astype(BF16)
    r1 = x - hi.astype(F32)
    mid = r1.astype(BF16)
    lo = (r1 - mid.astype(F32)).astype(BF16)
    return hi, mid, lo


def _dot_exact(a_bf16, x, dims=None, lhs_is_x=False):
    out = None
    for p in _split3(x):
        if dims is None:
            t = jnp.dot(p, a_bf16, preferred_element_type=F32) if lhs_is_x else jnp.dot(a_bf16, p, preferred_element_type=F32)
        else:
            t = lax.dot_general(p, a_bf16, dims, preferred_element_type=F32) if lhs_is_x else lax.dot_general(a_bf16, p, dims, preferred_element_type=F32)
        out = t if out is None else out + t
    return out


def _layer_norm(x, g, b):
    mu = jnp.mean(x, axis=-1, keepdims=True)
    xc = x - mu
    var = jnp.mean(xc * xc, axis=-1, keepdims=True)
    return xc * lax.rsqrt(var + LN_EPS) * g + b


ADA_TILE = 512


def _ada_body(c_ref, w_ref, b_ref, o_ref):
    c = c_ref[...]
    cs = c * jax.nn.sigmoid(c)
    o_ref[...] = jnp.dot(cs, w_ref[...], preferred_element_type=F32, precision=lax.Precision.HIGHEST) + b_ref[...]


def _ada(cpad, w, b):
    rows, d = cpad.shape
    n = w.shape[1]
    return pl.pallas_call(
        _ada_body, grid=(n // ADA_TILE,),
        in_specs=[pl.BlockSpec((rows, d), lambda j: (0, 0)),
                  pl.BlockSpec((d, ADA_TILE), lambda j: (0, j)),
                  pl.BlockSpec((1, ADA_TILE), lambda j: (0, j))],
        out_specs=pl.BlockSpec((rows, ADA_TILE), lambda j: (0, j)),
        out_shape=jax.ShapeDtypeStruct((rows, n), F32),
        compiler_params=_cparams(("arbitrary",)), name="ada",
    )(cpad, w, b)


def _modmm_body(mod_ref, x_ref, sc_ref, sh_ref, w_ref, o_ref, *, tn):
    h = (x_ref[...] * (1.0 + sc_ref[0]) + sh_ref[0]).astype(BF16)
    for j in range(w_ref.shape[1] // tn):
        o_ref[:, j * tn:(j + 1) * tn] = jnp.dot(h, w_ref[:, j * tn:(j + 1) * tn], preferred_element_type=F32)


def _mm_body(x_ref, w_ref, o_ref):
    o_ref[...] = jnp.dot(x_ref[...].astype(BF16), w_ref[...], preferred_element_type=F32)


def _modmm(x, sc, sh, w, mod_idx, n_tiles, tn):
    d = x.shape[1]
    n = w.shape[1]
    grid_spec = pltpu.PrefetchScalarGridSpec(
        num_scalar_prefetch=1,
        grid=(n_tiles,),
        in_specs=[
            pl.BlockSpec((TOK_TILE, d), lambda i, m: (i, 0)),
            pl.BlockSpec((1, 1, d), lambda i, m: (m[i], 0, 0)),
            pl.BlockSpec((1, 1, d), lambda i, m: (m[i], 0, 0)),
            pl.BlockSpec((d, n), lambda i, m: (0, 0), pipeline_mode=pl.Buffered(1)),
        ],
        out_specs=pl.BlockSpec((TOK_TILE, n), lambda i, m: (i, 0)),
    )
    return pl.pallas_call(
        functools.partial(_modmm_body, tn=tn), grid_spec=grid_spec,
        out_shape=jax.ShapeDtypeStruct((n_tiles * TOK_TILE, n), F32),
        compiler_params=_cparams(("arbitrary",), vmem_mb=56), name="modmm",
    )(mod_idx, x, sc, sh, w)


def _mm(x, w, n_tiles, tn):
    d = x.shape[1]
    n = w.shape[1]
    return pl.pallas_call(
        _mm_body, grid=(n // tn, n_tiles),
        in_specs=[pl.BlockSpec((TOK_TILE, d), lambda j, i: (i, 0)),
                  pl.BlockSpec((d, tn), lambda j, i: (0, j))],
        out_specs=pl.BlockSpec((TOK_TILE, tn), lambda j, i: (i, j)),
        out_shape=jax.ShapeDtypeStruct((n_tiles * TOK_TILE, n), F32),
        compiler_params=_cparams(("arbitrary", "arbitrary")), name="peer_q",
    )(x, w)


def _head_masks(width, per_head, heads):
    lane = lax.broadcasted_iota(I32, (1, width), 1)
    return [(lane >= h * per_head) & (lane < (h + 1) * per_head) for h in range(heads)]


def _na_body(q_ref, k_ref, v_ref, kc_ref, vc_ref, bias_ref, o_ref, *, rows):
    r = pl.program_id(1)
    r0 = jnp.clip(r - NA_WIN_H // 2, 0, rows - NA_WIN_H)
    delta = r - r0
    start = pl.multiple_of(r0 * GRID_W, GRID_W)
    nwin = NA_WIN_H * GRID_W
    kwin = k_ref[pl.ds(start, nwin), :].astype(BF16)
    vwin = v_ref[pl.ds(start, nwin), :].astype(BF16)
    kc = kc_ref[...].astype(BF16)
    vc = vc_ref[...].astype(BF16)
    q = q_ref[...] * (NA_HEAD_DIM ** -0.5)
    out = jnp.zeros(q.shape, F32)
    for h, hm in enumerate(_head_masks(BRANCH_W, NA_HEAD_DIM, NA_HEADS)):
        qh = jnp.where(hm, q, 0.0).astype(BF16)
        s = lax.dot_general(qh, kwin, _NT, preferred_element_type=F32) + bias_ref[h, delta]
        sc = lax.dot_general(qh, kc, _NT, preferred_element_type=F32)
        m = jnp.maximum(jnp.max(s, axis=1, keepdims=True), jnp.max(sc, axis=1, keepdims=True))
        e = jnp.exp(s - m)
        ec = jnp.exp(sc - m)
        den = jnp.sum(e, axis=1, keepdims=True) + jnp.sum(ec, axis=1, keepdims=True)
        oh = (jnp.dot(e.astype(BF16), vwin, preferred_element_type=F32)
              + jnp.dot(ec.astype(BF16), vc, preferred_element_type=F32))
        out = out + jnp.where(hm, oh / den, 0.0)
    o_ref[...] = out


def _na_bias_table(rpb):
    colv = np.arange(GRID_W)
    c0 = np.clip(colv - NA_WIN_W // 2, 0, GRID_W - NA_WIN_W)
    in_win = (colv[None, :] >= c0[:, None]) & (colv[None, :] < c0[:, None] + NA_WIN_W)
    edge = GRID_W - NA_WIN_W
    padded = jnp.pad(rpb, ((0, 0), (0, 0), (edge, edge)), mode="edge")
    cols = jnp.stack([padded[:, :, GRID_W - 1 - q:2 * GRID_W - 1 - q] for q in range(GRID_W)], axis=2)
    cols = jnp.where(jnp.asarray(in_win)[None, None], cols, NEG_INF)
    b = jnp.stack([cols[:, NA_WIN_H - 1 - d:2 * NA_WIN_H - 1 - d] for d in range(NA_WIN_H)], axis=1)
    b = b.transpose(0, 1, 3, 2, 4)
    return b.reshape(NA_HEADS, NA_WIN_H, GRID_W, NA_WIN_H * GRID_W).astype(F32)


def _na_attention(z, bias, B, T, L):
    rows = T // GRID_W
    cq, ck, cv = Z_NA // BRANCH_W, Z_NA // BRANCH_W + 1, Z_NA // BRANCH_W + 2
    ctx0 = (B * T) // L
    return pl.pallas_call(
        functools.partial(_na_body, rows=rows), grid=(B, rows),
        in_specs=[
            pl.BlockSpec((GRID_W, BRANCH_W), lambda b, r: (b * rows + r, cq)),
            pl.BlockSpec((T, BRANCH_W), lambda b, r: (b, ck)),
            pl.BlockSpec((T, BRANCH_W), lambda b, r: (b, cv)),
            pl.BlockSpec((L, BRANCH_W), lambda b, r: (ctx0 + b, ck)),
            pl.BlockSpec((L, BRANCH_W), lambda b, r: (ctx0 + b, cv)),
            pl.BlockSpec(bias.shape, lambda b, r: (0, 0, 0, 0)),
        ],
        out_specs=pl.BlockSpec((GRID_W, BRANCH_W), lambda b, r: (b * rows + r, 0)),
        out_shape=jax.ShapeDtypeStruct((B * T, BRANCH_W), F32),
        compiler_params=_cparams(("arbitrary", "arbitrary")), name="na_attn",
    )(z, z, z, z, z, bias)


def _ctx_attn_body(q_ref, k_ref, v_ref, o_ref):
    k = k_ref[...].astype(BF16)
    v = v_ref[...].astype(BF16)
    q = q_ref[...] * (NA_HEAD_DIM ** -0.5)
    out = jnp.zeros(q.shape, F32)
    for hm in _head_masks(BRANCH_W, NA_HEAD_DIM, NA_HEADS):
        qh = jnp.where(hm, q, 0.0).astype(BF16)
        s = lax.dot_general(qh, k, _NT, preferred_element_type=F32)
        m = jnp.max(s, axis=1, keepdims=True)
        e = jnp.exp(s - m)
        den = jnp.sum(e, axis=1, keepdims=True)
        oh = jnp.dot(e.astype(BF16), v, preferred_element_type=F32)
        out = out + jnp.where(hm, oh / den, 0.0)
    o_ref[...] = out


def _ctx_attention(z, B, T, L):
    cq = Z_NA // BRANCH_W
    ctx0 = (B * T) // L
    return pl.pallas_call(
        _ctx_attn_body, grid=(B,),
        in_specs=[pl.BlockSpec((L, BRANCH_W), lambda b, c=c: (ctx0 + b, cq + c)) for c in range(3)],
        out_specs=pl.BlockSpec((L, BRANCH_W), lambda b: (b, 0)),
        out_shape=jax.ShapeDtypeStruct((B * L, BRANCH_W), F32),
        compiler_params=_cparams(("arbitrary",)), name="ctx_attn",
    )(z, z, z)


def _gla_body(z_ref, cos_ref, sin_ref, gup_ref, gb_ref, s0_ref, o_ref, sfin_ref, state_ref, *, reverse, n_steps):
    g = pl.program_id(1)

    @pl.when(g == 0)
    def _():
        state_ref[...] = s0_ref[0]

    kw = GLA_HEADS * GLA_DK
    vw = GLA_HEADS * GLA_DV
    q = z_ref[:, 0:kw] * (GLA_DK ** -0.5)
    k = z_ref[:, kw:2 * kw]
    v = z_ref[:, 2 * kw:2 * kw + vw]
    lo = z_ref[:, 2 * kw + 2 * vw:2 * kw + 2 * vw + LANES]
    cos = cos_ref[...]
    sin = sin_ref[...]
    lane = lax.broadcasted_iota(I32, (1, kw), 1)
    first = (lane % (GLA_DK // 2)) < (GLA_DK // 4)

    def rope(x):
        partner = jnp.where(first, pltpu.roll(x, kw - GLA_DK // 4, 1), pltpu.roll(x, GLA_DK // 4, 1))
        return x * cos + partner * sin

    q = rope(q)
    k = rope(k)
    logits = jnp.dot(lo.astype(BF16), gup_ref[...], preferred_element_type=F32) + gb_ref[...]
    la = (jnp.minimum(logits, 0.0) - jnp.log1p(jnp.exp(-jnp.abs(logits)))) / GLA_TAU

    C = GLA_CHUNK
    ri = lax.broadcasted_iota(I32, (C, C), 0)
    ci = lax.broadcasted_iota(I32, (C, C), 1)
    tri = (ri <= ci) if reverse else (ri >= ci)
    cum = jnp.where(tri, 1.0, 0.0).astype(BF16)
    tri4 = jnp.concatenate([tri] * GLA_HEADS, axis=0)
    hm_k = _head_masks(kw, GLA_DK, GLA_HEADS)
    hm_v = _head_masks(vw, GLA_DV, GLA_HEADS)
    srow = lax.broadcasted_iota(I32, (kw, vw), 0) // GLA_DK
    scol = lax.broadcasted_iota(I32, (kw, vw), 1) // GLA_DV
    blockmask = srow == scol
    ones_cv = jnp.ones((C, vw), BF16)

    S = state_ref[...]
    n_chunks = z_ref.shape[0] // C
    order = range(n_chunks - 1, -1, -1) if reverse else range(n_chunks)
    for c in order:
        sl = slice(c * C, (c + 1) * C)
        la_c = la[sl]
        b = _dot_exact(cum, la_c)
        bl = b[0:1] if reverse else b[C - 1:C]
        qs = q[sl] * jnp.exp(b)
        ks = k[sl] * jnp.exp(-b)
        ke = k[sl] * jnp.exp(bl - b)
        vb = v[sl].astype(BF16)
        qs_b = qs.astype(BF16)
        qstack = jnp.concatenate([jnp.where(hm, qs, 0.0) for hm in hm_k], axis=0).astype(BF16)
        a = lax.dot_general(qstack, ks.astype(BF16), _NT, preferred_element_type=F32)
        a = jnp.where(tri4, a, 0.0)
        o_stack = jnp.dot(a.astype(BF16), vb, preferred_element_type=F32)
        o_c = jnp.dot(qs_b, S.astype(BF16), preferred_element_type=F32)
        for h, hm in enumerate(hm_v):
            o_c = o_c + jnp.where(hm, o_stack[h * C:(h + 1) * C], 0.0)
        o_ref[sl, :] = o_c
        u = lax.dot_general(ke.astype(BF16), vb, _TN, preferred_element_type=F32)
        dcol = _dot_exact(ones_cv, la_c, dims=_TN, lhs_is_x=True)
        S = jnp.exp(dcol) * S + jnp.where(blockmask, u, 0.0)
    state_ref[...] = S

    @pl.when(g == n_steps - 1)
    def _():
        sfin_ref[0] = S


def _gla_scan(z, cos, sin, gup, gb, s0, *, B, seq, row0_blocks, tile, reverse):
    n_steps = seq // tile
    kw = GLA_HEADS * GLA_DK
    vw = GLA_HEADS * GLA_DV

    def step(g):
        return n_steps - 1 - g if reverse else g

    return pl.pallas_call(
        functools.partial(_gla_body, reverse=reverse, n_steps=n_steps), grid=(B, n_steps),
        in_specs=[
            pl.BlockSpec((tile, 1024), lambda b, g: (row0_blocks + b * n_steps + step(g), Z_GLA // 1024)),
            pl.BlockSpec((tile, kw), lambda b, g: (step(g), 0)),
            pl.BlockSpec((tile, kw), lambda b, g: (step(g), 0)),
            pl.BlockSpec((LANES, kw), lambda b, g: (0, 0)),
            pl.BlockSpec((1, kw), lambda b, g: (0, 0)),
            pl.BlockSpec((1, kw, vw), lambda b, g: (b, 0, 0)),
        ],
        out_specs=[
            pl.BlockSpec((tile, vw), lambda b, g: (b * n_steps + step(g), 0)),
            pl.BlockSpec((1, kw, vw), lambda b, g: (b, 0, 0)),
        ],
        out_shape=[jax.ShapeDtypeStruct((B * seq, vw), F32), jax.ShapeDtypeStruct((B, kw, vw), F32)],
        scratch_shapes=[pltpu.VMEM((kw, vw), F32)],
        compiler_params=_cparams(("arbitrary", "arbitrary")), name="gla_rev" if reverse else "gla_fwd",
    )(z, cos, sin, gup, gb, s0)


def _rope_tables(T):
    t = np.arange(T)
    row, col = t // GRID_W, t % GRID_W
    nf = GLA_DK // 4
    inv = 1.0 / (ROPE_BASE ** (jnp.arange(nf, dtype=F32) / nf))
    j = np.arange(GLA_HEADS * GLA_DK)
    d = j % GLA_DK
    use_col = (d // (GLA_DK // 2)) == 1
    e = d % (GLA_DK // 2)
    fi = e % nf
    pos = jnp.where(jnp.asarray(use_col)[None, :], jnp.asarray(col, F32)[:, None], jnp.asarray(row, F32)[:, None])
    ang = pos * inv[fi][None, :]
    sign = jnp.asarray(np.where(e < nf, -1.0, 1.0), F32)[None, :]
    return jnp.cos(ang), jnp.sin(ang) * sign


CONV_PAD = 16
CONV_ROWS = 128


def _conv_body(z_ref, dw_ref, b_ref, g_ref, be_ref, o_ref, ypad_ref, *, seq):
    zeros = jnp.zeros((CONV_PAD, BRANCH_W), F32)
    ypad_ref[0:CONV_PAD, :] = zeros
    ypad_ref[seq + CONV_PAD:seq + 2 * CONV_PAD, :] = zeros

    def glu(i, carry):
        base = pl.multiple_of(i * CONV_ROWS, CONV_ROWS)
        a = z_ref[pl.ds(base, CONV_ROWS), 0:BRANCH_W]
        gate = z_ref[pl.ds(base, CONV_ROWS), BRANCH_W:2 * BRANCH_W]
        ypad_ref[pl.ds(base + CONV_PAD, CONV_ROWS), :] = a * jax.nn.sigmoid(gate)
        return carry

    lax.fori_loop(0, seq // CONV_ROWS, glu, 0)

    def tile(i, carry):
        base = pl.multiple_of(i * CONV_ROWS, CONV_ROWS)
        acc = jnp.zeros((CONV_ROWS, BRANCH_W), F32)
        win = ypad_ref[pl.ds(base, CONV_ROWS + 2 * CONV_PAD), :]
        for j in range(CONV_WIDTH):
            off = CONV_PAD - CONV_WIDTH // 2 + j
            acc = acc + win[off:off + CONV_ROWS] * dw_ref[j:j + 1, :]
        y = _layer_norm(acc + b_ref[...], g_ref[...], be_ref[...])
        o_ref[pl.ds(base, CONV_ROWS), :] = y * jax.nn.sigmoid(y)
        return carry

    lax.fori_loop(0, seq // CONV_ROWS, tile, 0)


def _conformer_conv(z, dw, b, g, be, *, B, seq, row0_blocks):
    vec = pl.BlockSpec((1, BRANCH_W), lambda i: (0, 0))
    return pl.pallas_call(
        functools.partial(_conv_body, seq=seq), grid=(B,),
        in_specs=[pl.BlockSpec((seq, 2 * BRANCH_W), lambda i: (row0_blocks + i, Z_CONV // (2 * BRANCH_W))),
                  pl.BlockSpec((CONV_WIDTH, BRANCH_W), lambda i: (0, 0)), vec, vec, vec],
        out_specs=pl.BlockSpec((seq, BRANCH_W), lambda i: (i, 0)),
        out_shape=jax.ShapeDtypeStruct((B * seq, BRANCH_W), F32),
        scratch_shapes=[pltpu.VMEM((seq + 2 * CONV_PAD, BRANCH_W), F32)],
        compiler_params=_cparams(("arbitrary",)), name="conformer_conv",
    )(z, dw, b, g, be)


def _sgu_body(z_ref, g_ref, b_ref, ws_ref, bs_ref, o_ref):
    zz = jax.nn.gelu(z_ref[...])
    u = zz[:, 0:BRANCH_W]
    v = _layer_norm(zz[:, BRANCH_W:2 * BRANCH_W], g_ref[...], b_ref[...])
    gms = _head_masks(BRANCH_W, BRANCH_W // SGU_GROUPS, SGU_GROUPS)
    for c in range(z_ref.shape[0] // SGU_CHUNK):
        sl = slice(c * SGU_CHUNK, (c + 1) * SGU_CHUNK)
        vb = v[sl].astype(BF16)
        s = bs_ref[...]
        for gi, gm in enumerate(gms):
            s = s + jnp.where(gm, jnp.dot(ws_ref[gi], vb, preferred_element_type=F32), 0.0)
        o_ref[sl, :] = u[sl] * s


def _spatial_gating(z, g, b, ws, bs_exp, n_tiles):
    vec = pl.BlockSpec((1, BRANCH_W), lambda i: (0, 0))
    return pl.pallas_call(
        _sgu_body, grid=(n_tiles,),
        in_specs=[pl.BlockSpec((TOK_TILE, 2 * BRANCH_W), lambda i: (i, Z_SGU // (2 * BRANCH_W))), vec, vec,
                  pl.BlockSpec(ws.shape, lambda i: (0, 0, 0)),
                  pl.BlockSpec((SGU_CHUNK, BRANCH_W), lambda i: (0, 0))],
        out_specs=pl.BlockSpec((TOK_TILE, BRANCH_W), lambda i: (i, 0)),
        out_shape=jax.ShapeDtypeStruct((n_tiles * TOK_TILE, BRANCH_W), F32),
        compiler_params=_cparams(("arbitrary",)), name="sgu",
    )(z, g, b, ws, bs_exp)


MERGE_TILE = 256


def _merge_body(mod_ref, gates_ref, yna_ref, of_ref, ob_ref, r_ref, ycv_ref, ysg_ref, x_ref, g1_ref, sc2_ref, sh2_ref,
                ng_ref, wb_ref, wo_ref, lg_ref, lb_ref, x1_ref, h2_ref):
    o = of_ref[...] + ob_ref[...]
    vw = GLA_HEADS * GLA_DV
    hr = lax.broadcasted_iota(I32, (vw, vw), 0) // GLA_DV
    hc = lax.broadcasted_iota(I32, (vw, vw), 1) // GLA_DV
    same_head = jnp.where(hr == hc, 1.0, 0.0).astype(BF16)
    ms = _dot_exact(same_head, o * o, lhs_is_x=True) * (1.0 / GLA_DV)
    r = r_ref[...]
    y_gla = o * lax.rsqrt(ms + LN_EPS) * ng_ref[...] * (r * jax.nn.sigmoid(r))
    ys = (yna_ref[...], y_gla, ycv_ref[...], ysg_ref[...])
    merged = None
    for i in range(4):
        gate = jax.nn.sigmoid(gates_ref[:, i * D_MODEL:(i + 1) * D_MODEL])
        term = gate * jnp.dot(ys[i].astype(BF16), wb_ref[i], preferred_element_type=F32)
        merged = term if merged is None else merged + term
    y = jnp.dot(merged.astype(BF16), wo_ref[...], preferred_element_type=F32)
    x1 = _layer_norm(ALPHA * x_ref[...] + g1_ref[0] * y, lg_ref[...], lb_ref[...])
    x1_ref[...] = x1
    h2_ref[...] = x1 * (1.0 + sc2_ref[0]) + sh2_ref[0]


def _merge(z, yna, of, ob, ycv, ysg, x, g1, sc2, sh2, ng, wb, wo, lg, lb, mod_idx, n_tiles):
    d = D_MODEL
    br = pl.BlockSpec((MERGE_TILE, BRANCH_W), lambda i, m: (i, 0))
    modv = pl.BlockSpec((1, 1, d), lambda i, m: (m[i], 0, 0))
    vec = pl.BlockSpec((1, d), lambda i, m: (0, 0))
    xt = pl.BlockSpec((MERGE_TILE, d), lambda i, m: (i, 0))
    grid_spec = pltpu.PrefetchScalarGridSpec(
        num_scalar_prefetch=1, grid=(n_tiles,),
        in_specs=[
            pl.BlockSpec((MERGE_TILE, 4 * d), lambda i, m: (i, 0)),
            br, br, br,
            pl.BlockSpec((MERGE_TILE, BRANCH_W), lambda i, m: (i, (Z_GLA + 512) // BRANCH_W)),
            br, br, xt, modv, modv, modv,
            pl.BlockSpec((1, BRANCH_W), lambda i, m: (0, 0)),
            pl.BlockSpec(wb.shape, lambda i, m: (0, 0, 0)),
            pl.BlockSpec(wo.shape, lambda i, m: (0, 0)),
            vec, vec,
        ],
        out_specs=[xt, xt],
    )
    return pl.pallas_call(
        _merge_body, grid_spec=grid_spec,
        out_shape=[jax.ShapeDtypeStruct((n_tiles * MERGE_TILE, d), F32)] * 2,
        compiler_params=_cparams(("arbitrary",)), name="merge",
    )(mod_idx, z, yna, of, ob, z, ycv, ysg, x, g1, sc2, sh2, ng, wb, wo, lg, lb)


TOPK_TILE = 256


def _top16(vals, payload=None):
    n_cand = vals.shape[0]
    pos_iota = lax.broadcasted_iota(I32, vals.shape, 0).astype(F32)
    out_v, out_p = [], []
    for _ in range(PEER_TOPK):
        m = jnp.max(vals, axis=0, keepdims=True)
        pos = jnp.min(jnp.where(vals == m, pos_iota, float(n_cand)), axis=0, keepdims=True)
        hit = pos_iota == pos
        out_v.append(m)
        out_p.append(pos if payload is None else jnp.max(jnp.where(hit, payload, -1.0), axis=0, keepdims=True))
        vals = jnp.where(hit, -jnp.inf, vals)
    return jnp.concatenate(out_v, axis=0), jnp.concatenate(out_p, axis=0)


_PAIR_COUNTS = [PEER_TOPK // (i + 1) for i in range(PEER_TOPK)]


TOPK_HEADS = 2


def _topk_body(q_ref, keys_ref, row_ref, g_ref):
    half = PEER_DQ // 2
    n_tok = q_ref.shape[0]
    for h in range(TOPK_HEADS):
        tops = []
        for s in range(2):
            qs = q_ref[:, (2 * h + s) * half:(2 * h + s + 1) * half].astype(BF16)
            sc = lax.dot_general(keys_ref[h, s], qs, _NT, preferred_element_type=F32)
            tops.append(_top16(sc))
        (a, ia), (b, ib) = tops
        cand = [a[i:i + 1] + b[0:n] for i, n in enumerate(_PAIR_COUNTS)]
        code = [ia[i:i + 1] * float(PEER_NKEYS) + ib[0:n] for i, n in enumerate(_PAIR_COUNTS)]
        pad = -sum(_PAIR_COUNTS) % SUBLANES
        cand = jnp.concatenate(cand + [jnp.full((pad, n_tok), -jnp.inf, F32)], axis=0)
        code = jnp.concatenate(code + [jnp.zeros((pad, n_tok), F32)], axis=0)
        best, expert = _top16(cand, code)
        e = jnp.exp(best - jnp.max(best, axis=0, keepdims=True))
        g_ref[h] = e / jnp.sum(e, axis=0, keepdims=True)
        row_ref[h] = expert.astype(I32) * HALF_SUB


def _peer_topk(q, keys, n_tiles):
    ntok = n_tiles * TOPK_TILE
    out = pl.BlockSpec((TOPK_HEADS, PEER_TOPK, TOPK_TILE), lambda i, h: (h, 0, i))
    return pl.pallas_call(
        _topk_body, grid=(n_tiles, PEER_HEADS // TOPK_HEADS),
        in_specs=[pl.BlockSpec((TOPK_TILE, TOPK_HEADS * PEER_DQ), lambda i, h: (i, h)),
                  pl.BlockSpec((TOPK_HEADS, 2, PEER_NKEYS, PEER_DQ // 2), lambda i, h: (h, 0, 0, 0))],
        out_specs=[out, out],
        out_shape=[jax.ShapeDtypeStruct((PEER_HEADS, PEER_TOPK, ntok), I32),
                   jax.ShapeDtypeStruct((PEER_HEADS, PEER_TOPK, ntok), F32)],
        compiler_params=_cparams(("arbitrary", "arbitrary")), name="peer_topk",
    )(q, keys)


HALF_SUB = SUBLANES // 2
N_SEL = PEER_HEADS * PEER_TOPK
SEL_ROWS = SUBLANES * N_SEL
U_LOW_ROWS, U_HIGH_ROWS = (7, 5, 3, 1), (6, 4, 2, 0)
V_LOW_ROWS, V_HIGH_ROWS = (0, 2, 4, 6), (1, 3, 5, 7)


PACK_TILE = 512


def _pack_body(t_ref, o_ref, *, low_rows, high_rows):
    def bf16_bits(x):
        return pltpu.bitcast(x.astype(BF16).astype(F32), I32)

    for s in range(HALF_SUB):
        low = bf16_bits(t_ref[:, low_rows[s] * LANES:(low_rows[s] + 1) * LANES])
        high = bf16_bits(t_ref[:, high_rows[s] * LANES:(high_rows[s] + 1) * LANES])
        o_ref[:, s, :] = lax.shift_right_logical(low, 16) | high


def _pack_table(tab, low_rows, high_rows):
    e, d = tab.shape
    packed = pl.pallas_call(
        functools.partial(_pack_body, low_rows=low_rows, high_rows=high_rows), grid=(e // PACK_TILE,),
        in_specs=[pl.BlockSpec((PACK_TILE, d), lambda i: (i, 0))],
        out_specs=pl.BlockSpec((PACK_TILE, HALF_SUB, LANES), lambda i: (i, 0, 0)),
        out_shape=jax.ShapeDtypeStruct((e, HALF_SUB, LANES), I32),
        compiler_params=_cparams(("arbitrary",)), name="pack_table",
    )(tab)
    return packed.reshape(e * HALF_SUB, LANES)


def _stage_rows(idx_ref, tab_ref, stage_ref, p):
    for k in range(N_SEL):
        off = pl.multiple_of(idx_ref.at[k][p], HALF_SUB)
        stage_ref[k * HALF_SUB:(k + 1) * HALF_SUB, :] = tab_ref[pl.ds(off, HALF_SUB), :]


N_SLOTS = 4


def _staged_token_loop(n_tok, stage, compute):
    for s in range(N_SLOTS):
        stage(s, s)

    def group(j, carry):
        p = N_SLOTS * j
        for s in range(N_SLOTS):
            compute(p + s, s)
            stage(jnp.minimum(p + s + N_SLOTS, n_tok - 1), s)
        return carry

    lax.fori_loop(0, n_tok // N_SLOTS, group, 0)


FOLD_LAG = SUBLANES


def _for_each_row_block(rows_hbm, row_bufs, sems, tb, run_block):
    g = pl.program_id(0)

    def copy(block, slot):
        return pltpu.make_async_copy(rows_hbm.at[:, pl.ds(block * tb, tb)], row_bufs[slot], sems.at[slot])

    @pl.when(g == 0)
    def _():
        copy(0, 0).start()

    copy(2 * g + 1, 1).start()
    copy(2 * g, 0).wait()
    run_block(row_bufs[0], 0)

    @pl.when(g + 1 < pl.num_programs(0))
    def _():
        copy(2 * g + 2, 0).start()

    copy(2 * g + 1, 1).wait()
    run_block(row_bufs[1], tb)


def _peer_u_body(rows_hbm, h_ref, g_ref, tab_ref, w_ref, rows0_ref, rows1_ref, sems, *scratch):
    stages, (y_ref, a_ref) = scratch[:N_SLOTS], scratch[N_SLOTS:]
    tb = h_ref.shape[0] // 2

    def fold(t):
        pieces = []
        for v in range(SEL_ROWS // LANES):
            z = pltpu.roll(y_ref[t, :, v * LANES:(v + 1) * LANES], LANES - SUBLANES + 1, 1, stride=1, stride_axis=0)
            pieces.append(jnp.sum(z, axis=0, keepdims=True))
        a_ref[pl.ds(t, 1), :] = jnp.concatenate(pieces, axis=1)

    def run_block(rows_ref, base):
        def compute(p, slot):
            fold(p)
            m = pltpu.bitcast(stages[slot][...], BF16)
            y_ref[p + FOLD_LAG] = lax.dot_general(h_ref[base + p].astype(BF16), m, _NT,
                                                  preferred_element_type=F32)

        y_ref[0:FOLD_LAG] = jnp.zeros((FOLD_LAG, SUBLANES, SEL_ROWS), F32)
        _staged_token_loop(tb, lambda p, slot: _stage_rows(rows_ref, tab_ref, stages[slot], p), compute)
        for t in range(tb, tb + FOLD_LAG):
            fold(t)
        w_ref[base:base + tb, :] = g_ref[base:base + tb, :] * jax.nn.gelu(a_ref[FOLD_LAG:tb + FOLD_LAG, :])

    _for_each_row_block(rows_hbm, (rows0_ref, rows1_ref), sems, tb, run_block)


def _peer_v_body(rows_hbm, w_ref, tab_ref, o_ref, rows0_ref, rows1_ref, sems, *stages):
    tb = o_ref.shape[0] // 2

    def run_block(rows_ref, base):
        def compute(p, slot):
            m = pltpu.bitcast(stages[slot][...], BF16)
            wrow = w_ref[pl.ds(base + p, 1), :]
            pieces = []
            for v in range(SEL_ROWS // LANES):
                wb = jnp.broadcast_to(wrow[:, v * LANES:(v + 1) * LANES], (SUBLANES, LANES))
                pieces.append(pltpu.roll(wb, 0, 1, stride=1, stride_axis=0))
            o_ref[base + p] = jnp.dot(jnp.concatenate(pieces, axis=1).astype(BF16), m,
                                      preferred_element_type=F32)

        _staged_token_loop(tb, lambda p, slot: _stage_rows(rows_ref, tab_ref, stages[slot], p), compute)

    _for_each_row_block(rows_hbm, (rows0_ref, rows1_ref), sems, tb, run_block)


def _table_spec(tab):
    return pl.BlockSpec(tab.shape, lambda i: (0, 0), pipeline_mode=pl.Buffered(1))


def _row_scratch(tb):
    return [pltpu.SMEM((N_SEL, tb), I32), pltpu.SMEM((N_SEL, tb), I32), pltpu.SemaphoreType.DMA((2,))]


def _peer_u(rows, h3, g_sparse, tab, n_tiles):
    tb = GATHER_TILE
    wide = pl.BlockSpec((2 * tb, SEL_ROWS), lambda i: (i, 0))
    return pl.pallas_call(
        _peer_u_body, grid=(n_tiles // 2,),
        in_specs=[pl.BlockSpec(memory_space=pl.ANY),
                  pl.BlockSpec((2 * tb, SUBLANES, LANES), lambda i: (i, 0, 0)), wide, _table_spec(tab)],
        out_specs=wide,
        out_shape=jax.ShapeDtypeStruct((n_tiles * tb, SEL_ROWS), F32),
        scratch_shapes=_row_scratch(tb) + [pltpu.VMEM((HALF_SUB * N_SEL, LANES), I32)] * N_SLOTS
        + [pltpu.VMEM((tb + FOLD_LAG, SUBLANES, SEL_ROWS), F32), pltpu.VMEM((tb + FOLD_LAG, SEL_ROWS), F32)],
        compiler_params=_cparams(("arbitrary",), vmem_mb=56), name="peer_u",
    )(rows, h3, g_sparse, tab)


def _peer_v(rows, w_sparse, tab, n_tiles):
    tb = GATHER_TILE
    return pl.pallas_call(
        _peer_v_body, grid=(n_tiles // 2,),
        in_specs=[pl.BlockSpec(memory_space=pl.ANY),
                  pl.BlockSpec((2 * tb, SEL_ROWS), lambda i: (i, 0)), _table_spec(tab)],
        out_specs=pl.BlockSpec((2 * tb, SUBLANES, LANES), lambda i: (i, 0, 0)),
        out_shape=jax.ShapeDtypeStruct((n_tiles * tb, SUBLANES, LANES), F32),
        scratch_shapes=_row_scratch(tb) + [pltpu.VMEM((HALF_SUB * N_SEL, LANES), I32)] * N_SLOTS,
        compiler_params=_cparams(("arbitrary",), vmem_mb=56), name="peer_v",
    )(rows, w_sparse, tab)


def _ln2_body(mod_ref, x_ref, f_ref, g2_ref, lg_ref, lb_ref, o_ref):
    o_ref[...] = _layer_norm(ALPHA * x_ref[...] + g2_ref[0] * f_ref[...], lg_ref[...], lb_ref[...])


def _ln2(x1, ffn, g2, lg, lb, mod_idx, n_tiles):
    d = D_MODEL
    xt = pl.BlockSpec((TOK_TILE, d), lambda i, m: (i, 0))
    vec = pl.BlockSpec((1, d), lambda i, m: (0, 0))
    grid_spec = pltpu.PrefetchScalarGridSpec(
        num_scalar_prefetch=1, grid=(n_tiles,),
        in_specs=[xt, xt, pl.BlockSpec((1, 1, d), lambda i, m: (m[i], 0, 0)), vec, vec],
        out_specs=xt,
    )
    return pl.pallas_call(
        _ln2_body, grid_spec=grid_spec,
        out_shape=jax.ShapeDtypeStruct((n_tiles * TOK_TILE, d), F32),
        compiler_params=_cparams(("arbitrary",)), name="ln2",
    )(mod_idx, x1, ffn, g2, lg, lb)


def _mod_index(B, T, L, tile, n_tiles):
    start = np.arange(n_tiles) * tile
    return jnp.asarray(np.where(start < B * T, start // T, B), I32)


def _pad_w_in(w_in):
    d = w_in.shape[0]
    na, gla, conv, sgu, gates = (w_in[:, 0:768], w_in[:, 768:1568], w_in[:, 1568:2080],
                                 w_in[:, 2080:2592], w_in[:, 2592:6688])
    z224 = jnp.zeros((d, 1024 - 800), w_in.dtype)
    z256 = jnp.zeros((d, 1024 - 768), w_in.dtype)
    return jnp.concatenate([gates, conv, sgu, gla, z224, na, z256], axis=1).astype(BF16)


def kernel(x, c, ctx, c_ctx, ada_w, ada_b, w_in, na_rpb, gla_gate_up, gla_gate_b, gla_norm_g, conv_dw, conv_b,
           conv_ln_g, conv_ln_b, sgu_ln_g, sgu_ln_b, sgu_ws, sgu_bs, w_branch, w_out, ln1_g, ln1_b, peer_wq,
           peer_keys, peer_u, peer_v, ln2_g, ln2_b):
    B, T, D = x.shape
    L = ctx.shape[1]
    depth = ada_w.shape[0]
    n_lat, n_ctx = B * T, B * L
    ntok = n_lat + n_ctx
    kw, vw = GLA_HEADS * GLA_DK, GLA_HEADS * GLA_DV

    xa = jnp.concatenate([x.reshape(n_lat, D), ctx.reshape(n_ctx, D)], axis=0)
    cos_l, sin_l = _rope_tables(T)
    cos_c, sin_c = jnp.ones((L, kw), F32), jnp.zeros((L, kw), F32)
    s_zero = jnp.zeros((B, kw, vw), F32)
    n_mod = -(-(B + 1) // SUBLANES) * SUBLANES
    cpad = jnp.concatenate([c, c_ctx[None, :], jnp.zeros((n_mod - B - 1, D), F32)], axis=0)

    for l in range(depth):
        need_ctx = l < depth - 1
        n_act = ntok if need_ctx else n_lat
        mod = _ada(cpad, ada_w[l], ada_b[l][None, :])
        sh1, sc1, g1, sh2, sc2, g2 = [m[:, None, :] for m in jnp.split(mod, 6, axis=-1)]

        z = _modmm(xa, sc1, sh1, _pad_w_in(w_in[l]), _mod_index(B, T, L, TOK_TILE, ntok // TOK_TILE),
                   ntok // TOK_TILE, 1024)

        y_na = _na_attention(z, _na_bias_table(na_rpb[l]), B, T, L)

        gups = []
        for d in range(2):
            gu = jnp.zeros((LANES, kw), F32).at[d * GLA_GATE_RANK:(d + 1) * GLA_GATE_RANK].set(gla_gate_up[l, d])
            gups.append(gu.astype(BF16))
        gbs = [gla_gate_b[l, d][None, :] for d in range(2)]
        ctx_blocks = n_lat // L
        oc_f, sc_f = _gla_scan(z, cos_c, sin_c, gups[0], gbs[0], s_zero, B=B, seq=L, row0_blocks=ctx_blocks,
                               tile=L, reverse=False)
        oc_b, sc_b = _gla_scan(z, cos_c, sin_c, gups[1], gbs[1], s_zero, B=B, seq=L, row0_blocks=ctx_blocks,
                               tile=L, reverse=True)
        o_f, _ = _gla_scan(z, cos_l, sin_l, gups[0], gbs[0], sc_f, B=B, seq=T, row0_blocks=0, tile=TOK_TILE,
                           reverse=False)
        o_b, _ = _gla_scan(z, cos_l, sin_l, gups[1], gbs[1], sc_b, B=B, seq=T, row0_blocks=0, tile=TOK_TILE,
                           reverse=True)

        cv_args = (conv_dw[l], conv_b[l][None, :], conv_ln_g[l][None, :], conv_ln_b[l][None, :])
        y_cv = _conformer_conv(z, *cv_args, B=B, seq=T, row0_blocks=0)
        bs_exp = jnp.repeat(sgu_bs[l].T, BRANCH_W // SGU_GROUPS, axis=1)
        y_sg = _spatial_gating(z, sgu_ln_g[l][None, :], sgu_ln_b[l][None, :], sgu_ws[l].astype(BF16), bs_exp,
                               n_act // TOK_TILE)

        if need_ctx:
            y_na = jnp.concatenate([y_na, _ctx_attention(z, B, T, L)], axis=0)
            o_f = jnp.concatenate([o_f, oc_f], axis=0)
            o_b = jnp.concatenate([o_b, oc_b], axis=0)
            y_cv = jnp.concatenate([y_cv, _conformer_conv(z, *cv_args, B=B, seq=L, row0_blocks=ctx_blocks)], axis=0)

        x1, h2 = _merge(z, y_na, o_f, o_b, y_cv, y_sg, xa, g1, sc2, sh2, gla_norm_g[l].reshape(1, vw),
                        w_branch[l].astype(BF16), w_out[l].astype(BF16), ln1_g[l][None, :], ln1_b[l][None, :],
                        _mod_index(B, T, L, MERGE_TILE, n_act // MERGE_TILE), n_act // MERGE_TILE)

        q = _mm(h2, peer_wq[l].astype(BF16), n_act // TOK_TILE, 1024)
        idx_t, g_t = _peer_topk(q, peer_keys[l].astype(BF16), n_act // TOPK_TILE)
        idx = idx_t.reshape(N_SEL, n_act)
        gate = g_t.reshape(N_SEL, n_act).T
        g_sparse = jnp.pad(gate[:, :, None], ((0, 0), (0, 0), (0, SUBLANES - 1))).reshape(n_act, SEL_ROWS)
        w_sparse = _peer_u(idx, h2.reshape(n_act, SUBLANES, LANES), g_sparse,
                           _pack_table(peer_u[l], U_LOW_ROWS, U_HIGH_ROWS), n_act // GATHER_TILE)
        ffn = _peer_v(idx, w_sparse, _pack_table(peer_v[l], V_LOW_ROWS, V_HIGH_ROWS),
                      n_act // GATHER_TILE).reshape(n_act, D)

        xa = _ln2(x1, ffn, g2, ln2_g[l][None, :], ln2_b[l][None, :],
                  _mod_index(B, T, L, TOK_TILE, n_act // TOK_TILE), n_act // TOK_TILE)

    return xa[:n_lat].reshape(B, T, D)
```

```python
import functools

import numpy as np
import jax
import jax.numpy as jnp
from jax import lax
from jax.experimental import pallas as pl
from jax.experimental.pallas import tpu as pltpu

F32 = jnp.float32
BF16 = jnp.bfloat16
I32 = jnp.int32

D_MODEL = 1024
GRID_W = 64
BRANCH_W = 256
NA_HEADS = 4
NA_HEAD_DIM = 64
NA_WIN_H = 8
NA_WIN_W = 16
GLA_HEADS = 4
GLA_DV = 64
GLA_DK = 32
GLA_GATE_RANK = 16
GLA_TAU = 16.0
GLA_CHUNK = 64
ROPE_BASE = 100.0
CONV_WIDTH = 31
SGU_GROUPS = 4
SGU_CHUNK = 128
PEER_HEADS = 8
PEER_NKEYS = 128
PEER_DQ = 256
PEER_TOPK = 16
DEPTH = 2
ALPHA = (2 * DEPTH) ** 0.25
NEG_INF = -1e30
LN_EPS = 1e-6

SUBLANES = 8
LANES = 128

Z_GATES = 0
Z_CONV = 4096
Z_SGU = 4608
Z_GLA = 5120
Z_NA = 6144
Z_COLS = 7168

TOK_TILE = 512
GATHER_TILE = 128

_NT = (((1,), (1,)), ((), ()))
_TN = (((0,), (0,)), ((), ()))


def _cparams(sem, vmem_mb=48):
    return pltpu.CompilerParams(dimension_semantics=sem, vmem_limit_bytes=vmem_mb * 1024 * 1024)


def _split3(x):
    hi = x.astype(BF16)
    r1 = x - hi.astype(F32)
    mid = r1.astype(BF16)
    lo = (r1 - mid.astype(F32)).astype(BF16)
    return hi, mid, lo


def _dot_exact(a_bf16, x, dims=None, lhs_is_x=False):
    out = None
    for p in _split3(x):
        if dims is None:
            t = jnp.dot(p, a_bf16, preferred_element_type=F32) if lhs_is_x else jnp.dot(a_bf16, p, preferred_element_type=F32)
        else:
            t = lax.dot_general(p, a_bf16, dims, preferred_element_type=F32) if lhs_is_x else lax.dot_general(a_bf16, p, dims, preferred_element_type=F32)
        out = t if out is None else out + t
    return out


def _layer_norm(x, g, b):
    mu = jnp.mean(x, axis=-1, keepdims=True)
    xc = x - mu
    var = jnp.mean(xc * xc, axis=-1, keepdims=True)
    return xc * lax.rsqrt(var + LN_EPS) * g + b


ADA_TILE = 512


def _ada_body(c_ref, w_ref, b_ref, o_ref):
    c = c_ref[...]
    cs = c * jax.nn.sigmoid(c)
    o_ref[...] = jnp.dot(cs, w_ref[...], preferred_element_type=F32, precision=lax.Precision.HIGHEST) + b_ref[...]


def _ada(cpad, w, b):
    rows, d = cpad.shape
    n = w.shape[1]
    return pl.pallas_call(
        _ada_body, grid=(n // ADA_TILE,),
        in_specs=[pl.BlockSpec((rows, d), lambda j: (0, 0)),
                  pl.BlockSpec((d, ADA_TILE), lambda j: (0, j)),
                  pl.BlockSpec((1, ADA_TILE), lambda j: (0, j))],
        out_specs=pl.BlockSpec((rows, ADA_TILE), lambda j: (0, j)),
        out_shape=jax.ShapeDtypeStruct((rows, n), F32),
        compiler_params=_cparams(("arbitrary",)), name="ada",
    )(cpad, w, b)


def _modmm_body(mod_ref, x_ref, sc_ref, sh_ref, w_ref, o_ref, *, tn):
    h = (x_ref[...] * (1.0 + sc_ref[0]) + sh_ref[0]).astype(BF16)
    for j in range(w_ref.shape[1] // tn):
        o_ref[:, j * tn:(j + 1) * tn] = jnp.dot(h, w_ref[:, j * tn:(j + 1) * tn], preferred_element_type=F32)


def _mm_body(x_ref, w_ref, o_ref):
    o_ref[...] = jnp.dot(x_ref[...].astype(BF16), w_ref[...], preferred_element_type=F32)


def _modmm(x, sc, sh, w, mod_idx, n_tiles, tn):
    d = x.shape[1]
    n = w.shape[1]
    grid_spec = pltpu.PrefetchScalarGridSpec(
        num_scalar_prefetch=1,
        grid=(n_tiles,),
        in_specs=[
            pl.BlockSpec((TOK_TILE, d), lambda i, m: (i, 0)),
            pl.BlockSpec((1, 1, d), lambda i, m: (m[i], 0, 0)),
            pl.BlockSpec((1, 1, d), lambda i, m: (m[i], 0, 0)),
            pl.BlockSpec((d, n), lambda i, m: (0, 0), pipeline_mode=pl.Buffered(1)),
        ],
        out_specs=pl.BlockSpec((TOK_TILE, n), lambda i, m: (i, 0)),
    )
    return pl.pallas_call(
        functools.partial(_modmm_body, tn=tn), grid_spec=grid_spec,
        out_shape=jax.ShapeDtypeStruct((n_tiles * TOK_TILE, n), F32),
        compiler_params=_cparams(("arbitrary",), vmem_mb=56), name="modmm",
    )(mod_idx, x, sc, sh, w)


def _mm(x, w, n_tiles, tn):
    d = x.shape[1]
    n = w.shape[1]
    return pl.pallas_call(
        _mm_body, grid=(n // tn, n_tiles),
        in_specs=[pl.BlockSpec((TOK_TILE, d), lambda j, i: (i, 0)),
                  pl.BlockSpec((d, tn), lambda j, i: (0, j))],
        out_specs=pl.BlockSpec((TOK_TILE, tn), lambda j, i: (i, j)),
        out_shape=jax.ShapeDtypeStruct((n_tiles * TOK_TILE, n), F32),
        compiler_params=_cparams(("arbitrary", "arbitrary")), name="peer_q",
    )(x, w)


def _head_masks(width, per_head, heads):
    lane = lax.broadcasted_iota(I32, (1, width), 1)
    return [(lane >= h * per_head) & (lane < (h + 1) * per_head) for h in range(heads)]


NA_ROWS_PER_STEP = 2


def _na_body(q_ref, k_ref, v_ref, kc_ref, vc_ref, bias_ref, o_ref, *, rows):
    nwin = NA_WIN_H * GRID_W
    kc = kc_ref[...].astype(BF16)
    vc = vc_ref[...].astype(BF16)
    masks = _head_masks(BRANCH_W, NA_HEAD_DIM, NA_HEADS)
    scores, vwins = [], []
    for i in range(NA_ROWS_PER_STEP):
        r = pl.program_id(1) * NA_ROWS_PER_STEP + i
        r0 = jnp.clip(r - NA_WIN_H // 2, 0, rows - NA_WIN_H)
        delta = r - r0
        start = pl.multiple_of(r0 * GRID_W, GRID_W)
        kwin = k_ref[pl.ds(start, nwin), :].astype(BF16)
        vwins.append(v_ref[pl.ds(start, nwin), :].astype(BF16))
        q = q_ref[i * GRID_W:(i + 1) * GRID_W, :] * (NA_HEAD_DIM ** -0.5)
        for h, hm in enumerate(masks):
            qh = jnp.where(hm, q, 0.0).astype(BF16)
            scores.append((lax.dot_general(qh, kwin, _NT, preferred_element_type=F32) + bias_ref[h, delta],
                           lax.dot_general(qh, kc, _NT, preferred_element_type=F32)))
    probs = []
    for s, sc in scores:
        m = jnp.maximum(jnp.max(s, axis=1, keepdims=True), jnp.max(sc, axis=1, keepdims=True))
        e = jnp.exp(s - m)
        ec = jnp.exp(sc - m)
        den = jnp.sum(e, axis=1, keepdims=True) + jnp.sum(ec, axis=1, keepdims=True)
        probs.append((e.astype(BF16), ec.astype(BF16), den))
    for i in range(NA_ROWS_PER_STEP):
        out = jnp.zeros((GRID_W, BRANCH_W), F32)
        for h, hm in enumerate(masks):
            e, ec, den = probs[i * NA_HEADS + h]
            oh = jnp.dot(e, vwins[i], preferred_element_type=F32) + jnp.dot(ec, vc, preferred_element_type=F32)
            out = out + jnp.where(hm, oh / den, 0.0)
        o_ref[i * GRID_W:(i + 1) * GRID_W, :] = out


def _na_bias_table(rpb):
    colv = np.arange(GRID_W)
    c0 = np.clip(colv - NA_WIN_W // 2, 0, GRID_W - NA_WIN_W)
    in_win = (colv[None, :] >= c0[:, None]) & (colv[None, :] < c0[:, None] + NA_WIN_W)
    edge = GRID_W - NA_WIN_W
    padded = jnp.pad(rpb, ((0, 0), (0, 0), (edge, edge)), mode="edge")
    cols = jnp.stack([padded[:, :, GRID_W - 1 - q:2 * GRID_W - 1 - q] for q in range(GRID_W)], axis=2)
    cols = jnp.where(jnp.asarray(in_win)[None, None], cols, NEG_INF)
    b = jnp.stack([cols[:, NA_WIN_H - 1 - d:2 * NA_WIN_H - 1 - d] for d in range(NA_WIN_H)], axis=1)
    b = b.transpose(0, 1, 3, 2, 4)
    return b.reshape(NA_HEADS, NA_WIN_H, GRID_W, NA_WIN_H * GRID_W).astype(F32)


def _na_attention(z, bias, B, T, L):
    rows = T // GRID_W
    steps, qrows = rows // NA_ROWS_PER_STEP, NA_ROWS_PER_STEP * GRID_W
    cq, ck, cv = Z_NA // BRANCH_W, Z_NA // BRANCH_W + 1, Z_NA // BRANCH_W + 2
    ctx0 = (B * T) // L
    return pl.pallas_call(
        functools.partial(_na_body, rows=rows), grid=(B, steps),
        in_specs=[
            pl.BlockSpec((qrows, BRANCH_W), lambda b, r: (b * steps + r, cq)),
            pl.BlockSpec((T, BRANCH_W), lambda b, r: (b, ck)),
            pl.BlockSpec((T, BRANCH_W), lambda b, r: (b, cv)),
            pl.BlockSpec((L, BRANCH_W), lambda b, r: (ctx0 + b, ck)),
            pl.BlockSpec((L, BRANCH_W), lambda b, r: (ctx0 + b, cv)),
            pl.BlockSpec(bias.shape, lambda b, r: (0, 0, 0, 0)),
        ],
        out_specs=pl.BlockSpec((qrows, BRANCH_W), lambda b, r: (b * steps + r, 0)),
        out_shape=jax.ShapeDtypeStruct((B * T, BRANCH_W), F32),
        compiler_params=_cparams(("arbitrary", "arbitrary")), name="na_attn",
    )(z, z, z, z, z, bias)


def _ctx_attn_body(q_ref, k_ref, v_ref, o_ref):
    k = k_ref[...].astype(BF16)
    v = v_ref[...].astype(BF16)
    q = q_ref[...] * (NA_HEAD_DIM ** -0.5)
    out = jnp.zeros(q.shape, F32)
    for hm in _head_masks(BRANCH_W, NA_HEAD_DIM, NA_HEADS):
        qh = jnp.where(hm, q, 0.0).astype(BF16)
        s = lax.dot_general(qh, k, _NT, preferred_element_type=F32)
        m = jnp.max(s, axis=1, keepdims=True)
        e = jnp.exp(s - m)
        den = jnp.sum(e, axis=1, keepdims=True)
        oh = jnp.dot(e.astype(BF16), v, preferred_element_type=F32)
        out = out + jnp.where(hm, oh / den, 0.0)
    o_ref[...] = out


def _ctx_attention(z, B, T, L):
    cq = Z_NA // BRANCH_W
    ctx0 = (B * T) // L
    return pl.pallas_call(
        _ctx_attn_body, grid=(B,),
        in_specs=[pl.BlockSpec((L, BRANCH_W), lambda b, c=c: (ctx0 + b, cq + c)) for c in range(3)],
        out_specs=pl.BlockSpec((L, BRANCH_W), lambda b: (b, 0)),
        out_shape=jax.ShapeDtypeStruct((B * L, BRANCH_W), F32),
        compiler_params=_cparams(("arbitrary",)), name="ctx_attn",
    )(z, z, z)


def _gla_body(z_ref, cos_ref, sin_ref, gup_ref, gb_ref, s0_ref, o_ref, sfin_ref, state_ref, *, reverse, n_steps):
    g = pl.program_id(1)

    @pl.when(g == 0)
    def _():
        state_ref[...] = s0_ref[0]

    kw = GLA_HEADS * GLA_DK
    vw = GLA_HEADS * GLA_DV
    q = z_ref[:, 0:kw] * (GLA_DK ** -0.5)
    k = z_ref[:, kw:2 * kw]
    v = z_ref[:, 2 * kw:2 * kw + vw]
    lo = z_ref[:, 2 * kw + 2 * vw:2 * kw + 2 * vw + LANES]
    cos = cos_ref[...]
    sin = sin_ref[...]
    lane = lax.broadcasted_iota(I32, (1, kw), 1)
    first = (lane % (GLA_DK // 2)) < (GLA_DK // 4)

    def rope(x):
        partner = jnp.where(first, pltpu.roll(x, kw - GLA_DK // 4, 1), pltpu.roll(x, GLA_DK // 4, 1))
        return x * cos + partner * sin

    q = rope(q)
    k = rope(k)
    logits = jnp.dot(lo.astype(BF16), gup_ref[...], preferred_element_type=F32) + gb_ref[...]
    la = (jnp.minimum(logits, 0.0) - jnp.log1p(jnp.exp(-jnp.abs(logits)))) / GLA_TAU

    C = GLA_CHUNK
    ri = lax.broadcasted_iota(I32, (C, C), 0)
    ci = lax.broadcasted_iota(I32, (C, C), 1)
    tri = (ri <= ci) if reverse else (ri >= ci)
    cum = jnp.where(tri, 1.0, 0.0).astype(BF16)
    tri4 = jnp.concatenate([tri] * GLA_HEADS, axis=0)
    hm_k = _head_masks(kw, GLA_DK, GLA_HEADS)
    hm_v = _head_masks(vw, GLA_DV, GLA_HEADS)
    srow = lax.broadcasted_iota(I32, (kw, vw), 0) // GLA_DK
    scol = lax.broadcasted_iota(I32, (kw, vw), 1) // GLA_DV
    blockmask = srow == scol
    ones_cv = jnp.ones((C, vw), BF16)

    S = state_ref[...]
    n_chunks = z_ref.shape[0] // C
    order = range(n_chunks - 1, -1, -1) if reverse else range(n_chunks)
    for c in order:
        sl = slice(c * C, (c + 1) * C)
        la_c = la[sl]
        b = _dot_exact(cum, la_c)
        bl = b[0:1] if reverse else b[C - 1:C]
        qs = q[sl] * jnp.exp(b)
        ks = k[sl] * jnp.exp(-b)
        ke = k[sl] * jnp.exp(bl - b)
        vb = v[sl].astype(BF16)
        qs_b = qs.astype(BF16)
        qstack = jnp.concatenate([jnp.where(hm, qs, 0.0) for hm in hm_k], axis=0).astype(BF16)
        a = lax.dot_general(qstack, ks.astype(BF16), _NT, preferred_element_type=F32)
        a = jnp.where(tri4, a, 0.0)
        o_stack = jnp.dot(a.astype(BF16), vb, preferred_element_type=F32)
        o_c = jnp.dot(qs_b, S.astype(BF16), preferred_element_type=F32)
        for h, hm in enumerate(hm_v):
            o_c = o_c + jnp.where(hm, o_stack[h * C:(h + 1) * C], 0.0)
        o_ref[sl, :] = o_c
        u = lax.dot_general(ke.astype(BF16), vb, _TN, preferred_element_type=F32)
        dcol = _dot_exact(ones_cv, la_c, dims=_TN, lhs_is_x=True)
        S = jnp.exp(dcol) * S + jnp.where(blockmask, u, 0.0)
    state_ref[...] = S

    @pl.when(g == n_steps - 1)
    def _():
        sfin_ref[0] = S


def _gla_scan(z, cos, sin, gup, gb, s0, *, B, seq, row0_blocks, tile, reverse):
    n_steps = seq // tile
    kw = GLA_HEADS * GLA_DK
    vw = GLA_HEADS * GLA_DV

    def step(g):
        return n_steps - 1 - g if reverse else g

    return pl.pallas_call(
        functools.partial(_gla_body, reverse=reverse, n_steps=n_steps), grid=(B, n_steps),
        in_specs=[
            pl.BlockSpec((tile, 1024), lambda b, g: (row0_blocks + b * n_steps + step(g), Z_GLA // 1024)),
            pl.BlockSpec((tile, kw), lambda b, g: (step(g), 0)),
            pl.BlockSpec((tile, kw), lambda b, g: (step(g), 0)),
            pl.BlockSpec((LANES, kw), lambda b, g: (0, 0)),
            pl.BlockSpec((1, kw), lambda b, g: (0, 0)),
            pl.BlockSpec((1, kw, vw), lambda b, g: (b, 0, 0)),
        ],
        out_specs=[
            pl.BlockSpec((tile, vw), lambda b, g: (b * n_steps + step(g), 0)),
            pl.BlockSpec((1, kw, vw), lambda b, g: (b, 0, 0)),
        ],
        out_shape=[jax.ShapeDtypeStruct((B * seq, vw), F32), jax.ShapeDtypeStruct((B, kw, vw), F32)],
        scratch_shapes=[pltpu.VMEM((kw, vw), F32)],
        compiler_params=_cparams(("arbitrary", "arbitrary")), name="gla_rev" if reverse else "gla_fwd",
    )(z, cos, sin, gup, gb, s0)


def _rope_tables(T):
    t = np.arange(T)
    row, col = t // GRID_W, t % GRID_W
    nf = GLA_DK // 4
    inv = 1.0 / (ROPE_BASE ** (jnp.arange(nf, dtype=F32) / nf))
    j = np.arange(GLA_HEADS * GLA_DK)
    d = j % GLA_DK
    use_col = (d // (GLA_DK // 2)) == 1
    e = d % (GLA_DK // 2)
    fi = e % nf
    pos = jnp.where(jnp.asarray(use_col)[None, :], jnp.asarray(col, F32)[:, None], jnp.asarray(row, F32)[:, None])
    ang = pos * inv[fi][None, :]
    sign = jnp.asarray(np.where(e < nf, -1.0, 1.0), F32)[None, :]
    return jnp.cos(ang), jnp.sin(ang) * sign


CONV_PAD = 16
CONV_ROWS = 128


def _conv_body(z_ref, dw_ref, b_ref, g_ref, be_ref, o_ref, ypad_ref, *, seq):
    zeros = jnp.zeros((CONV_PAD, BRANCH_W), F32)
    ypad_ref[0:CONV_PAD, :] = zeros
    ypad_ref[seq + CONV_PAD:seq + 2 * CONV_PAD, :] = zeros

    def glu(i, carry):
        base = pl.multiple_of(i * CONV_ROWS, CONV_ROWS)
        a = z_ref[pl.ds(base, CONV_ROWS), 0:BRANCH_W]
        gate = z_ref[pl.ds(base, CONV_ROWS), BRANCH_W:2 * BRANCH_W]
        ypad_ref[pl.ds(base + CONV_PAD, CONV_ROWS), :] = a * jax.nn.sigmoid(gate)
        return carry

    lax.fori_loop(0, seq // CONV_ROWS, glu, 0)

    def tile(i, carry):
        base = pl.multiple_of(i * CONV_ROWS, CONV_ROWS)
        acc = jnp.zeros((CONV_ROWS, BRANCH_W), F32)
        win = ypad_ref[pl.ds(base, CONV_ROWS + 2 * CONV_PAD), :]
        for j in range(CONV_WIDTH):
            off = CONV_PAD - CONV_WIDTH // 2 + j
            acc = acc + win[off:off + CONV_ROWS] * dw_ref[j:j + 1, :]
        y = _layer_norm(acc + b_ref[...], g_ref[...], be_ref[...])
        o_ref[pl.ds(base, CONV_ROWS), :] = y * jax.nn.sigmoid(y)
        return carry

    lax.fori_loop(0, seq // CONV_ROWS, tile, 0)


def _conformer_conv(z, dw, b, g, be, *, B, seq, row0_blocks):
    vec = pl.BlockSpec((1, BRANCH_W), lambda i: (0, 0))
    return pl.pallas_call(
        functools.partial(_conv_body, seq=seq), grid=(B,),
        in_specs=[pl.BlockSpec((seq, 2 * BRANCH_W), lambda i: (row0_blocks + i, Z_CONV // (2 * BRANCH_W))),
                  pl.BlockSpec((CONV_WIDTH, BRANCH_W), lambda i: (0, 0)), vec, vec, vec],
        out_specs=pl.BlockSpec((seq, BRANCH_W), lambda i: (i, 0)),
        out_shape=jax.ShapeDtypeStruct((B * seq, BRANCH_W), F32),
        scratch_shapes=[pltpu.VMEM((seq + 2 * CONV_PAD, BRANCH_W), F32)],
        compiler_params=_cparams(("arbitrary",)), name="conformer_conv",
    )(z, dw, b, g, be)


def _sgu_body(z_ref, g_ref, b_ref, ws_ref, bs_ref, o_ref):
    zz = jax.nn.gelu(z_ref[...])
    u = zz[:, 0:BRANCH_W]
    v = _layer_norm(zz[:, BRANCH_W:2 * BRANCH_W], g_ref[...], b_ref[...])
    gms = _head_masks(BRANCH_W, BRANCH_W // SGU_GROUPS, SGU_GROUPS)
    for c in range(z_ref.shape[0] // SGU_CHUNK):
        sl = slice(c * SGU_CHUNK, (c + 1) * SGU_CHUNK)
        vb = v[sl].astype(BF16)
        s = bs_ref[...]
        for gi, gm in enumerate(gms):
            s = s + jnp.where(gm, jnp.dot(ws_ref[gi], vb, preferred_element_type=F32), 0.0)
        o_ref[sl, :] = u[sl] * s


def _spatial_gating(z, g, b, ws, bs_exp, n_tiles):
    vec = pl.BlockSpec((1, BRANCH_W), lambda i: (0, 0))
    return pl.pallas_call(
        _sgu_body, grid=(n_tiles,),
        in_specs=[pl.BlockSpec((TOK_TILE, 2 * BRANCH_W), lambda i: (i, Z_SGU // (2 * BRANCH_W))), vec, vec,
                  pl.BlockSpec(ws.shape, lambda i: (0, 0, 0)),
                  pl.BlockSpec((SGU_CHUNK, BRANCH_W), lambda i: (0, 0))],
        out_specs=pl.BlockSpec((TOK_TILE, BRANCH_W), lambda i: (i, 0)),
        out_shape=jax.ShapeDtypeStruct((n_tiles * TOK_TILE, BRANCH_W), F32),
        compiler_params=_cparams(("arbitrary",)), name="sgu",
    )(z, g, b, ws, bs_exp)


MERGE_TILE = 256


def _merge_body(mod_ref, gates_ref, yna_ref, of_ref, ob_ref, r_ref, ycv_ref, ysg_ref, x_ref, g1_ref, sc2_ref, sh2_ref,
                ng_ref, wb_ref, wo_ref, lg_ref, lb_ref, x1_ref, h2_ref):
    o = of_ref[...] + ob_ref[...]
    vw = GLA_HEADS * GLA_DV
    hr = lax.broadcasted_iota(I32, (vw, vw), 0) // GLA_DV
    hc = lax.broadcasted_iota(I32, (vw, vw), 1) // GLA_DV
    same_head = jnp.where(hr == hc, 1.0, 0.0).astype(BF16)
    ms = _dot_exact(same_head, o * o, lhs_is_x=True) * (1.0 / GLA_DV)
    r = r_ref[...]
    y_gla = o * lax.rsqrt(ms + LN_EPS) * ng_ref[...] * (r * jax.nn.sigmoid(r))
    ys = (yna_ref[...], y_gla, ycv_ref[...], ysg_ref[...])
    merged = None
    for i in range(4):
        gate = jax.nn.sigmoid(gates_ref[:, i * D_MODEL:(i + 1) * D_MODEL])
        term = gate * jnp.dot(ys[i].astype(BF16), wb_ref[i], preferred_element_type=F32)
        merged = term if merged is None else merged + term
    y = jnp.dot(merged.astype(BF16), wo_ref[...], preferred_element_type=F32)
    x1 = _layer_norm(ALPHA * x_ref[...] + g1_ref[0] * y, lg_ref[...], lb_ref[...])
    x1_ref[...] = x1
    h2_ref[...] = x1 * (1.0 + sc2_ref[0]) + sh2_ref[0]


def _merge(z, yna, of, ob, ycv, ysg, x, g1, sc2, sh2, ng, wb, wo, lg, lb, mod_idx, n_tiles):
    d = D_MODEL
    br = pl.BlockSpec((MERGE_TILE, BRANCH_W), lambda i, m: (i, 0))
    modv = pl.BlockSpec((1, 1, d), lambda i, m: (m[i], 0, 0))
    vec = pl.BlockSpec((1, d), lambda i, m: (0, 0))
    xt = pl.BlockSpec((MERGE_TILE, d), lambda i, m: (i, 0))
    grid_spec = pltpu.PrefetchScalarGridSpec(
        num_scalar_prefetch=1, grid=(n_tiles,),
        in_specs=[
            pl.BlockSpec((MERGE_TILE, 4 * d), lambda i, m: (i, 0)),
            br, br, br,
            pl.BlockSpec((MERGE_TILE, BRANCH_W), lambda i, m: (i, (Z_GLA + 512) // BRANCH_W)),
            br, br, xt, modv, modv, modv,
            pl.BlockSpec((1, BRANCH_W), lambda i, m: (0, 0)),
            pl.BlockSpec(wb.shape, lambda i, m: (0, 0, 0)),
            pl.BlockSpec(wo.shape, lambda i, m: (0, 0)),
            vec, vec,
        ],
        out_specs=[xt, xt],
    )
    return pl.pallas_call(
        _merge_body, grid_spec=grid_spec,
        out_shape=[jax.ShapeDtypeStruct((n_tiles * MERGE_TILE, d), F32)] * 2,
        compiler_params=_cparams(("arbitrary",)), name="merge",
    )(mod_idx, z, yna, of, ob, z, ycv, ysg, x, g1, sc2, sh2, ng, wb, wo, lg, lb)


TOPK_TILE = 256


def _top16(vals, payload=None):
    n_cand = vals.shape[0]
    pos_iota = lax.broadcasted_iota(I32, vals.shape, 0).astype(F32)
    out_v, out_p = [], []
    for _ in range(PEER_TOPK):
        m = jnp.max(vals, axis=0, keepdims=True)
        pos = jnp.min(jnp.where(vals == m, pos_iota, float(n_cand)), axis=0, keepdims=True)
        hit = pos_iota == pos
        out_v.append(m)
        out_p.append(pos if payload is None else jnp.max(jnp.where(hit, payload, -1.0), axis=0, keepdims=True))
        vals = jnp.where(hit, -jnp.inf, vals)
    return jnp.concatenate(out_v, axis=0), jnp.concatenate(out_p, axis=0)


_PAIR_COUNTS = [PEER_TOPK // (i + 1) for i in range(PEER_TOPK)]


TOPK_HEADS = 2


def _topk_body(q_ref, keys_ref, row_ref, g_ref):
    half = PEER_DQ // 2
    n_tok = q_ref.shape[0]
    for h in range(TOPK_HEADS):
        tops = []
        for s in range(2):
            qs = q_ref[:, (2 * h + s) * half:(2 * h + s + 1) * half].astype(BF16)
            sc = lax.dot_general(keys_ref[h, s], qs, _NT, preferred_element_type=F32)
            tops.append(_top16(sc))
        (a, ia), (b, ib) = tops
        cand = [a[i:i + 1] + b[0:n] for i, n in enumerate(_PAIR_COUNTS)]
        code = [ia[i:i + 1] * float(PEER_NKEYS) + ib[0:n] for i, n in enumerate(_PAIR_COUNTS)]
        pad = -sum(_PAIR_COUNTS) % SUBLANES
        cand = jnp.concatenate(cand + [jnp.full((pad, n_tok), -jnp.inf, F32)], axis=0)
        code = jnp.concatenate(code + [jnp.zeros((pad, n_tok), F32)], axis=0)
        best, expert = _top16(cand, code)
        e = jnp.exp(best - jnp.max(best, axis=0, keepdims=True))
        g_ref[h] = e / jnp.sum(e, axis=0, keepdims=True)
        row_ref[h] = expert.astype(I32) * HALF_SUB


def _peer_topk(q, keys, n_tiles):
    ntok = n_tiles * TOPK_TILE
    out = pl.BlockSpec((TOPK_HEADS, PEER_TOPK, TOPK_TILE), lambda i, h: (h, 0, i))
    return pl.pallas_call(
        _topk_body, grid=(n_tiles, PEER_HEADS // TOPK_HEADS),
        in_specs=[pl.BlockSpec((TOPK_TILE, TOPK_HEADS * PEER_DQ), lambda i, h: (i, h)),
                  pl.BlockSpec((TOPK_HEADS, 2, PEER_NKEYS, PEER_DQ // 2), lambda i, h: (h, 0, 0, 0))],
        out_specs=[out, out],
        out_shape=[jax.ShapeDtypeStruct((PEER_HEADS, PEER_TOPK, ntok), I32),
                   jax.ShapeDtypeStruct((PEER_HEADS, PEER_TOPK, ntok), F32)],
        compiler_params=_cparams(("arbitrary", "arbitrary")), name="peer_topk",
    )(q, keys)


HALF_SUB = SUBLANES // 2
N_SEL = PEER_HEADS * PEER_TOPK
SEL_ROWS = SUBLANES * N_SEL
U_LOW_ROWS, U_HIGH_ROWS = (7, 5, 3, 1), (6, 4, 2, 0)
V_LOW_ROWS, V_HIGH_ROWS = (0, 2, 4, 6), (1, 3, 5, 7)


PACK_TILE = 512


def _pack_body(t_ref, o_ref, *, low_rows, high_rows):
    def bf16_bits(x):
        return pltpu.bitcast(x.astype(BF16).astype(F32), I32)

    for s in range(HALF_SUB):
        low = bf16_bits(t_ref[:, low_rows[s] * LANES:(low_rows[s] + 1) * LANES])
        high = bf16_bits(t_ref[:, high_rows[s] * LANES:(high_rows[s] + 1) * LANES])
        o_ref[:, s, :] = lax.shift_right_logical(low, 16) | high


def _pack_table(tab, low_rows, high_rows):
    e, d = tab.shape
    packed = pl.pallas_call(
        functools.partial(_pack_body, low_rows=low_rows, high_rows=high_rows), grid=(e // PACK_TILE,),
        in_specs=[pl.BlockSpec((PACK_TILE, d), lambda i: (i, 0))],
        out_specs=pl.BlockSpec((PACK_TILE, HALF_SUB, LANES), lambda i: (i, 0, 0)),
        out_shape=jax.ShapeDtypeStruct((e, HALF_SUB, LANES), I32),
        compiler_params=_cparams(("arbitrary",)), name="pack_table",
    )(tab)
    return packed.reshape(e * HALF_SUB, LANES)


def _stage_rows(idx_ref, tab_ref, stage_ref, p):
    for k in range(N_SEL):
        off = pl.multiple_of(idx_ref.at[k][p], HALF_SUB)
        stage_ref[k * HALF_SUB:(k + 1) * HALF_SUB, :] = tab_ref[pl.ds(off, HALF_SUB), :]


N_SLOTS = 4


def _staged_token_loop(n_tok, stage, compute):
    for s in range(N_SLOTS):
        stage(s, s)

    def group(j, carry):
        p = N_SLOTS * j
        for s in range(N_SLOTS):
            compute(p + s, s)
            stage(jnp.minimum(p + s + N_SLOTS, n_tok - 1), s)
        return carry

    lax.fori_loop(0, n_tok // N_SLOTS, group, 0)


FOLD_LAG = SUBLANES


def _for_each_row_block(rows_hbm, row_bufs, sems, tb, run_block):
    g = pl.program_id(0)

    def copy(block, slot):
        return pltpu.make_async_copy(rows_hbm.at[:, pl.ds(block * tb, tb)], row_bufs[slot], sems.at[slot])

    @pl.when(g == 0)
    def _():
        copy(0, 0).start()

    copy(2 * g + 1, 1).start()
    copy(2 * g, 0).wait()
    run_block(row_bufs[0], 0)

    @pl.when(g + 1 < pl.num_programs(0))
    def _():
        copy(2 * g + 2, 0).start()

    copy(2 * g + 1, 1).wait()
    run_block(row_bufs[1], tb)


def _peer_u_body(rows_hbm, h_ref, g_ref, tab_ref, w_ref, rows0_ref, rows1_ref, sems, *scratch):
    stages, (y_ref, a_ref) = scratch[:N_SLOTS], scratch[N_SLOTS:]
    tb = h_ref.shape[0] // 2

    def fold(t):
        pieces = []
        for v in range(SEL_ROWS // LANES):
            z = pltpu.roll(y_ref[t, :, v * LANES:(v + 1) * LANES], LANES - SUBLANES + 1, 1, stride=1, stride_axis=0)
            pieces.append(jnp.sum(z, axis=0, keepdims=True))
        a_ref[pl.ds(t, 1), :] = jnp.concatenate(pieces, axis=1)

    def run_block(rows_ref, base):
        def compute(p, slot):
            fold(p)
            m = pltpu.bitcast(stages[slot][...], BF16)
            y_ref[p + FOLD_LAG] = lax.dot_general(h_ref[base + p].astype(BF16), m, _NT,
                                                  preferred_element_type=F32)

        y_ref[0:FOLD_LAG] = jnp.zeros((FOLD_LAG, SUBLANES, SEL_ROWS), F32)
        _staged_token_loop(tb, lambda p, slot: _stage_rows(rows_ref, tab_ref, stages[slot], p), compute)
        for t in range(tb, tb + FOLD_LAG):
            fold(t)

        def gate_rows(c, carry):
            lo = pl.multiple_of(c * SUBLANES, SUBLANES)
            rows = pl.ds(base + lo, SUBLANES)
            w_ref[rows, :] = g_ref[rows, :] * jax.nn.gelu(a_ref[pl.ds(FOLD_LAG + lo, SUBLANES), :])
            return carry

        lax.fori_loop(0, tb // SUBLANES, gate_rows, 0)

    _for_each_row_block(rows_hbm, (rows0_ref, rows1_ref), sems, tb, run_block)


def _peer_v_body(rows_hbm, w_ref, tab_ref, o_ref, rows0_ref, rows1_ref, sems, *stages):
    tb = o_ref.shape[0] // 2

    def run_block(rows_ref, base):
        def compute(p, slot):
            m = pltpu.bitcast(stages[slot][...], BF16)
            wrow = w_ref[pl.ds(base + p, 1), :]
            pieces = []
            for v in range(SEL_ROWS // LANES):
                wb = jnp.broadcast_to(wrow[:, v * LANES:(v + 1) * LANES], (SUBLANES, LANES))
                pieces.append(pltpu.roll(wb, 0, 1, stride=1, stride_axis=0))
            out = jnp.dot(jnp.concatenate(pieces, axis=1).astype(BF16), m, preferred_element_type=F32)
            o_ref[pl.ds(base + p, 1), :] = jnp.concatenate([out[v:v + 1] for v in range(SUBLANES)], axis=1)

        _staged_token_loop(tb, lambda p, slot: _stage_rows(rows_ref, tab_ref, stages[slot], p), compute)

    _for_each_row_block(rows_hbm, (rows0_ref, rows1_ref), sems, tb, run_block)


def _table_spec(tab):
    return pl.BlockSpec(tab.shape, lambda i: (0, 0), pipeline_mode=pl.Buffered(1))


def _row_scratch(tb):
    return [pltpu.SMEM((N_SEL, tb), I32), pltpu.SMEM((N_SEL, tb), I32), pltpu.SemaphoreType.DMA((2,))]


def _peer_u(rows, h3, g_sparse, tab, n_tiles):
    tb = GATHER_TILE
    wide = pl.BlockSpec((2 * tb, SEL_ROWS), lambda i: (i, 0))
    return pl.pallas_call(
        _peer_u_body, grid=(n_tiles // 2,),
        in_specs=[pl.BlockSpec(memory_space=pl.ANY),
                  pl.BlockSpec((2 * tb, SUBLANES, LANES), lambda i: (i, 0, 0)), wide, _table_spec(tab)],
        out_specs=wide,
        out_shape=jax.ShapeDtypeStruct((n_tiles * tb, SEL_ROWS), F32),
        scratch_shapes=_row_scratch(tb) + [pltpu.VMEM((HALF_SUB * N_SEL, LANES), I32)] * N_SLOTS
        + [pltpu.VMEM((tb + FOLD_LAG, SUBLANES, SEL_ROWS), F32), pltpu.VMEM((tb + FOLD_LAG, SEL_ROWS), F32)],
        compiler_params=_cparams(("arbitrary",), vmem_mb=56), name="peer_u",
    )(rows, h3, g_sparse, tab)


def _peer_v(rows, w_sparse, tab, n_tiles):
    tb = GATHER_TILE
    return pl.pallas_call(
        _peer_v_body, grid=(n_tiles // 2,),
        in_specs=[pl.BlockSpec(memory_space=pl.ANY),
                  pl.BlockSpec((2 * tb, SEL_ROWS), lambda i: (i, 0)), _table_spec(tab)],
        out_specs=pl.BlockSpec((2 * tb, SEL_ROWS), lambda i: (i, 0)),
        out_shape=jax.ShapeDtypeStruct((n_tiles * tb, SEL_ROWS), F32),
        scratch_shapes=_row_scratch(tb) + [pltpu.VMEM((HALF_SUB * N_SEL, LANES), I32)] * N_SLOTS,
        compiler_params=_cparams(("arbitrary",), vmem_mb=56), name="peer_v",
    )(rows, w_sparse, tab)


def _ln2_body(mod_ref, x_ref, f_ref, g2_ref, lg_ref, lb_ref, o_ref):
    o_ref[...] = _layer_norm(ALPHA * x_ref[...] + g2_ref[0] * f_ref[...], lg_ref[...], lb_ref[...])


def _ln2(x1, ffn, g2, lg, lb, mod_idx, n_tiles):
    d = D_MODEL
    xt = pl.BlockSpec((TOK_TILE, d), lambda i, m: (i, 0))
    vec = pl.BlockSpec((1, d), lambda i, m: (0, 0))
    grid_spec = pltpu.PrefetchScalarGridSpec(
        num_scalar_prefetch=1, grid=(n_tiles,),
        in_specs=[xt, xt, pl.BlockSpec((1, 1, d), lambda i, m: (m[i], 0, 0)), vec, vec],
        out_specs=xt,
    )
    return pl.pallas_call(
        _ln2_body, grid_spec=grid_spec,
        out_shape=jax.ShapeDtypeStruct((n_tiles * TOK_TILE, d), F32),
        compiler_params=_cparams(("arbitrary",)), name="ln2",
    )(mod_idx, x1, ffn, g2, lg, lb)


def _mod_index(B, T, L, tile, n_tiles):
    start = np.arange(n_tiles) * tile
    return jnp.asarray(np.where(start < B * T, start // T, B), I32)


def _pad_w_in(w_in):
    d = w_in.shape[0]
    na, gla, conv, sgu, gates = (w_in[:, 0:768], w_in[:, 768:1568], w_in[:, 1568:2080],
                                 w_in[:, 2080:2592], w_in[:, 2592:6688])
    z224 = jnp.zeros((d, 1024 - 800), w_in.dtype)
    z256 = jnp.zeros((d, 1024 - 768), w_in.dtype)
    return jnp.concatenate([gates, conv, sgu, gla, z224, na, z256], axis=1).astype(BF16)


def kernel(x, c, ctx, c_ctx, ada_w, ada_b, w_in, na_rpb, gla_gate_up, gla_gate_b, gla_norm_g, conv_dw, conv_b,
           conv_ln_g, conv_ln_b, sgu_ln_g, sgu_ln_b, sgu_ws, sgu_bs, w_branch, w_out, ln1_g, ln1_b, peer_wq,
           peer_keys, peer_u, peer_v, ln2_g, ln2_b):
    B, T, D = x.shape
    L = ctx.shape[1]
    depth = ada_w.shape[0]
    n_lat, n_ctx = B * T, B * L
    ntok = n_lat + n_ctx
    kw, vw = GLA_HEADS * GLA_DK, GLA_HEADS * GLA_DV

    xa = jnp.concatenate([x.reshape(n_lat, D), ctx.reshape(n_ctx, D)], axis=0)
    cos_l, sin_l = _rope_tables(T)
    cos_c, sin_c = jnp.ones((L, kw), F32), jnp.zeros((L, kw), F32)
    s_zero = jnp.zeros((B, kw, vw), F32)
    n_mod = -(-(B + 1) // SUBLANES) * SUBLANES
    cpad = jnp.concatenate([c, c_ctx[None, :], jnp.zeros((n_mod - B - 1, D), F32)], axis=0)

    for l in range(depth):
        need_ctx = l < depth - 1
        n_act = ntok if need_ctx else n_lat
        mod = _ada(cpad, ada_w[l], ada_b[l][None, :])
        sh1, sc1, g1, sh2, sc2, g2 = [m[:, None, :] for m in jnp.split(mod, 6, axis=-1)]

        z = _modmm(xa, sc1, sh1, _pad_w_in(w_in[l]), _mod_index(B, T, L, TOK_TILE, ntok // TOK_TILE),
                   ntok // TOK_TILE, 1024)

        y_na = _na_attention(z, _na_bias_table(na_rpb[l]), B, T, L)

        gups = []
        for d in range(2):
            gu = jnp.zeros((LANES, kw), F32).at[d * GLA_GATE_RANK:(d + 1) * GLA_GATE_RANK].set(gla_gate_up[l, d])
            gups.append(gu.astype(BF16))
        gbs = [gla_gate_b[l, d][None, :] for d in range(2)]
        ctx_blocks = n_lat // L
        oc_f, sc_f = _gla_scan(z, cos_c, sin_c, gups[0], gbs[0], s_zero, B=B, seq=L, row0_blocks=ctx_blocks,
                               tile=L, reverse=False)
        oc_b, sc_b = _gla_scan(z, cos_c, sin_c, gups[1], gbs[1], s_zero, B=B, seq=L, row0_blocks=ctx_blocks,
                               tile=L, reverse=True)
        o_f, _ = _gla_scan(z, cos_l, sin_l, gups[0], gbs[0], sc_f, B=B, seq=T, row0_blocks=0, tile=TOK_TILE,
                           reverse=False)
        o_b, _ = _gla_scan(z, cos_l, sin_l, gups[1], gbs[1], sc_b, B=B, seq=T, row0_blocks=0, tile=TOK_TILE,
                           reverse=True)

        cv_args = (conv_dw[l], conv_b[l][None, :], conv_ln_g[l][None, :], conv_ln_b[l][None, :])
        y_cv = _conformer_conv(z, *cv_args, B=B, seq=T, row0_blocks=0)
        bs_exp = jnp.repeat(sgu_bs[l].T, BRANCH_W // SGU_GROUPS, axis=1)
        y_sg = _spatial_gating(z, sgu_ln_g[l][None, :], sgu_ln_b[l][None, :], sgu_ws[l].astype(BF16), bs_exp,
                               n_act // TOK_TILE)

        if need_ctx:
            y_na = jnp.concatenate([y_na, _ctx_attention(z, B, T, L)], axis=0)
            o_f = jnp.concatenate([o_f, oc_f], axis=0)
            o_b = jnp.concatenate([o_b, oc_b], axis=0)
            y_cv = jnp.concatenate([y_cv, _conformer_conv(z, *cv_args, B=B, seq=L, row0_blocks=ctx_blocks)], axis=0)

        x1, h2 = _merge(z, y_na, o_f, o_b, y_cv, y_sg, xa, g1, sc2, sh2, gla_norm_g[l].reshape(1, vw),
                        w_branch[l].astype(BF16), w_out[l].astype(BF16), ln1_g[l][None, :], ln1_b[l][None, :],
                        _mod_index(B, T, L, MERGE_TILE, n_act // MERGE_TILE), n_act // MERGE_TILE)

        q = _mm(h2, peer_wq[l].astype(BF16), n_act // TOK_TILE, 1024)
        idx_t, g_t = _peer_topk(q, peer_keys[l].astype(BF16), n_act // TOPK_TILE)
        idx = idx_t.reshape(N_SEL, n_act)
        gate = g_t.reshape(N_SEL, n_act).T
        g_sparse = jnp.pad(gate[:, :, None], ((0, 0), (0, 0), (0, SUBLANES - 1))).reshape(n_act, SEL_ROWS)
        w_sparse = _peer_u(idx, h2.reshape(n_act, SUBLANES, LANES), g_sparse,
                           _pack_table(peer_u[l], U_LOW_ROWS, U_HIGH_ROWS), n_act // GATHER_TILE)
        ffn = _peer_v(idx, w_sparse, _pack_table(peer_v[l], V_LOW_ROWS, V_HIGH_ROWS), n_act // GATHER_TILE)

        xa = _ln2(x1, ffn, g2, ln2_g[l][None, :], ln2_b[l][None, :],
                  _mod_index(B, T, L, TOK_TILE, n_act // TOK_TILE), n_act // TOK_TILE)

    return xa[:n_lat].reshape(B, T, D)
```

```python
import functools

import numpy as np
import jax
import jax.numpy as jnp
from jax import lax
from jax.experimental import pallas as pl
from jax.experimental.pallas import tpu as pltpu

F32 = jnp.float32
BF16 = jnp.bfloat16
I32 = jnp.int32

D_MODEL = 1024
GRID_W = 64
BRANCH_W = 256
NA_HEADS = 4
NA_HEAD_DIM = 64
NA_WIN_H = 8
NA_WIN_W = 16
GLA_HEADS = 4
GLA_DV = 64
GLA_DK = 32
GLA_GATE_RANK = 16
GLA_TAU = 16.0
GLA_CHUNK = 64
ROPE_BASE = 100.0
CONV_WIDTH = 31
SGU_GROUPS = 4
SGU_CHUNK = 128
PEER_HEADS = 8
PEER_NKEYS = 128
PEER_DQ = 256
PEER_TOPK = 16
DEPTH = 2
ALPHA = (2 * DEPTH) ** 0.25
NEG_INF = -1e30
LN_EPS = 1e-6

SUBLANES = 8
LANES = 128

Z_GATES = 0
Z_CONV = 4096
Z_SGU = 4608
Z_GLA = 5120
Z_NA = 6144
Z_COLS = 7168

TOK_TILE = 512
GATHER_TILE = 128

_NT = (((1,), (1,)), ((), ()))
_TN = (((0,), (0,)), ((), ()))


def _cparams(sem, vmem_mb=48):
    return pltpu.CompilerParams(dimension_semantics=sem, vmem_limit_bytes=vmem_mb * 1024 * 1024)


def _split3(x):
    hi = x.astype(BF16)
    r1 = x - hi.astype(F32)
    mid = r1.astype(BF16)
    lo = (r1 - mid.astype(F32)).astype(BF16)
    return hi, mid, lo


def _dot_exact(a_bf16, x, dims=None, lhs_is_x=False):
    out = None
    for p in _split3(x):
        if dims is None:
            t = jnp.dot(p, a_bf16, preferred_element_type=F32) if lhs_is_x else jnp.dot(a_bf16, p, preferred_element_type=F32)
        else:
            t = lax.dot_general(p, a_bf16, dims, preferred_element_type=F32) if lhs_is_x else lax.dot_general(a_bf16, p, dims, preferred_element_type=F32)
        out = t if out is None else out + t
    return out


def _layer_norm(x, g, b):
    mu = jnp.mean(x, axis=-1, keepdims=True)
    xc = x - mu
    var = jnp.mean(xc * xc, axis=-1, keepdims=True)
    return xc * lax.rsqrt(var + LN_EPS) * g + b


ADA_TILE = 512


def _ada_body(c_ref, w_ref, b_ref, o_ref):
    c = c_ref[...]
    cs = c * jax.nn.sigmoid(c)
    o_ref[...] = jnp.dot(cs, w_ref[...], preferred_element_type=F32, precision=lax.Precision.HIGHEST) + b_ref[...]


def _ada(cpad, w, b):
    rows, d = cpad.shape
    n = w.shape[1]
    return pl.pallas_call(
        _ada_body, grid=(n // ADA_TILE,),
        in_specs=[pl.BlockSpec((rows, d), lambda j: (0, 0)),
                  pl.BlockSpec((d, ADA_TILE), lambda j: (0, j)),
                  pl.BlockSpec((1, ADA_TILE), lambda j: (0, j))],
        out_specs=pl.BlockSpec((rows, ADA_TILE), lambda j: (0, j)),
        out_shape=jax.ShapeDtypeStruct((rows, n), F32),
        compiler_params=_cparams(("arbitrary",)), name="ada",
    )(cpad, w, b)


def _modmm_body(mod_ref, x_ref, sc_ref, sh_ref, w_ref, o_ref, *, tn):
    h = (x_ref[...] * (1.0 + sc_ref[0]) + sh_ref[0]).astype(BF16)
    for j in range(w_ref.shape[1] // tn):
        o_ref[:, j * tn:(j + 1) * tn] = jnp.dot(h, w_ref[:, j * tn:(j + 1) * tn], preferred_element_type=F32)


def _modmm(x, sc, sh, w, mod_idx, n_tiles, tn):
    d = x.shape[1]
    n = w.shape[1]
    grid_spec = pltpu.PrefetchScalarGridSpec(
        num_scalar_prefetch=1,
        grid=(n_tiles,),
        in_specs=[
            pl.BlockSpec((TOK_TILE, d), lambda i, m: (i, 0)),
            pl.BlockSpec((1, 1, d), lambda i, m: (m[i], 0, 0)),
            pl.BlockSpec((1, 1, d), lambda i, m: (m[i], 0, 0)),
            pl.BlockSpec((d, n), lambda i, m: (0, 0), pipeline_mode=pl.Buffered(1)),
        ],
        out_specs=pl.BlockSpec((TOK_TILE, n), lambda i, m: (i, 0)),
    )
    return pl.pallas_call(
        functools.partial(_modmm_body, tn=tn), grid_spec=grid_spec,
        out_shape=jax.ShapeDtypeStruct((n_tiles * TOK_TILE, n), F32),
        compiler_params=_cparams(("arbitrary",), vmem_mb=56), name="modmm",
    )(mod_idx, x, sc, sh, w)


def _head_masks(width, per_head, heads):
    lane = lax.broadcasted_iota(I32, (1, width), 1)
    return [(lane >= h * per_head) & (lane < (h + 1) * per_head) for h in range(heads)]


NA_ROWS_PER_STEP = 2


def _na_body(q_ref, k_ref, v_ref, kc_ref, vc_ref, bias_ref, o_ref, *, rows):
    nwin = NA_WIN_H * GRID_W
    kc = kc_ref[...].astype(BF16)
    vc = vc_ref[...].astype(BF16)
    masks = _head_masks(BRANCH_W, NA_HEAD_DIM, NA_HEADS)
    scores, vwins = [], []
    for i in range(NA_ROWS_PER_STEP):
        r = pl.program_id(1) * NA_ROWS_PER_STEP + i
        r0 = jnp.clip(r - NA_WIN_H // 2, 0, rows - NA_WIN_H)
        delta = r - r0
        start = pl.multiple_of(r0 * GRID_W, GRID_W)
        kwin = k_ref[pl.ds(start, nwin), :].astype(BF16)
        vwins.append(v_ref[pl.ds(start, nwin), :].astype(BF16))
        q = q_ref[i * GRID_W:(i + 1) * GRID_W, :] * (NA_HEAD_DIM ** -0.5)
        for h, hm in enumerate(masks):
            qh = jnp.where(hm, q, 0.0).astype(BF16)
            scores.append((lax.dot_general(qh, kwin, _NT, preferred_element_type=F32) + bias_ref[h, delta],
                           lax.dot_general(qh, kc, _NT, preferred_element_type=F32)))
    probs = []
    for s, sc in scores:
        m = jnp.maximum(jnp.max(s, axis=1, keepdims=True), jnp.max(sc, axis=1, keepdims=True))
        e = jnp.exp(s - m)
        ec = jnp.exp(sc - m)
        den = jnp.sum(e, axis=1, keepdims=True) + jnp.sum(ec, axis=1, keepdims=True)
        probs.append((e.astype(BF16), ec.astype(BF16), den))
    for i in range(NA_ROWS_PER_STEP):
        out = jnp.zeros((GRID_W, BRANCH_W), F32)
        for h, hm in enumerate(masks):
            e, ec, den = probs[i * NA_HEADS + h]
            oh = jnp.dot(e, vwins[i], preferred_element_type=F32) + jnp.dot(ec, vc, preferred_element_type=F32)
            out = out + jnp.where(hm, oh / den, 0.0)
        o_ref[i * GRID_W:(i + 1) * GRID_W, :] = out


def _na_bias_table(rpb):
    colv = np.arange(GRID_W)
    c0 = np.clip(colv - NA_WIN_W // 2, 0, GRID_W - NA_WIN_W)
    in_win = (colv[None, :] >= c0[:, None]) & (colv[None, :] < c0[:, None] + NA_WIN_W)
    edge = GRID_W - NA_WIN_W
    padded = jnp.pad(rpb, ((0, 0), (0, 0), (edge, edge)), mode="edge")
    cols = jnp.stack([padded[:, :, GRID_W - 1 - q:2 * GRID_W - 1 - q] for q in range(GRID_W)], axis=2)
    cols = jnp.where(jnp.asarray(in_win)[None, None], cols, NEG_INF)
    b = jnp.stack([cols[:, NA_WIN_H - 1 - d:2 * NA_WIN_H - 1 - d] for d in range(NA_WIN_H)], axis=1)
    b = b.transpose(0, 1, 3, 2, 4)
    return b.reshape(NA_HEADS, NA_WIN_H, GRID_W, NA_WIN_H * GRID_W).astype(F32)


def _na_attention(z, bias, B, T, L):
    rows = T // GRID_W
    steps, qrows = rows // NA_ROWS_PER_STEP, NA_ROWS_PER_STEP * GRID_W
    cq, ck, cv = Z_NA // BRANCH_W, Z_NA // BRANCH_W + 1, Z_NA // BRANCH_W + 2
    ctx0 = (B * T) // L
    return pl.pallas_call(
        functools.partial(_na_body, rows=rows), grid=(B, steps),
        in_specs=[
            pl.BlockSpec((qrows, BRANCH_W), lambda b, r: (b * steps + r, cq)),
            pl.BlockSpec((T, BRANCH_W), lambda b, r: (b, ck)),
            pl.BlockSpec((T, BRANCH_W), lambda b, r: (b, cv)),
            pl.BlockSpec((L, BRANCH_W), lambda b, r: (ctx0 + b, ck)),
            pl.BlockSpec((L, BRANCH_W), lambda b, r: (ctx0 + b, cv)),
            pl.BlockSpec(bias.shape, lambda b, r: (0, 0, 0, 0)),
        ],
        out_specs=pl.BlockSpec((qrows, BRANCH_W), lambda b, r: (b * steps + r, 0)),
        out_shape=jax.ShapeDtypeStruct((B * T, BRANCH_W), F32),
        compiler_params=_cparams(("arbitrary", "arbitrary")), name="na_attn",
    )(z, z, z, z, z, bias)


def _ctx_attn_body(q_ref, k_ref, v_ref, o_ref):
    k = k_ref[...].astype(BF16)
    v = v_ref[...].astype(BF16)
    q = q_ref[...] * (NA_HEAD_DIM ** -0.5)
    out = jnp.zeros(q.shape, F32)
    for hm in _head_masks(BRANCH_W, NA_HEAD_DIM, NA_HEADS):
        qh = jnp.where(hm, q, 0.0).astype(BF16)
        s = lax.dot_general(qh, k, _NT, preferred_element_type=F32)
        m = jnp.max(s, axis=1, keepdims=True)
        e = jnp.exp(s - m)
        den = jnp.sum(e, axis=1, keepdims=True)
        oh = jnp.dot(e.astype(BF16), v, preferred_element_type=F32)
        out = out + jnp.where(hm, oh / den, 0.0)
    o_ref[...] = out


def _ctx_attention(z, B, T, L):
    cq = Z_NA // BRANCH_W
    ctx0 = (B * T) // L
    return pl.pallas_call(
        _ctx_attn_body, grid=(B,),
        in_specs=[pl.BlockSpec((L, BRANCH_W), lambda b, c=c: (ctx0 + b, cq + c)) for c in range(3)],
        out_specs=pl.BlockSpec((L, BRANCH_W), lambda b: (b, 0)),
        out_shape=jax.ShapeDtypeStruct((B * L, BRANCH_W), F32),
        compiler_params=_cparams(("arbitrary",)), name="ctx_attn",
    )(z, z, z)


def _gla_body(z_ref, cos_ref, sin_ref, gup_ref, gb_ref, s0_ref, o_ref, sfin_ref, state_ref, *, reverse, n_steps):
    g = pl.program_id(1)

    @pl.when(g == 0)
    def _():
        state_ref[...] = s0_ref[0]

    kw = GLA_HEADS * GLA_DK
    vw = GLA_HEADS * GLA_DV
    q = z_ref[:, 0:kw] * (GLA_DK ** -0.5)
    k = z_ref[:, kw:2 * kw]
    v = z_ref[:, 2 * kw:2 * kw + vw]
    lo = z_ref[:, 2 * kw + 2 * vw:2 * kw + 2 * vw + LANES]
    cos = cos_ref[...]
    sin = sin_ref[...]
    lane = lax.broadcasted_iota(I32, (1, kw), 1)
    first = (lane % (GLA_DK // 2)) < (GLA_DK // 4)

    def rope(x):
        partner = jnp.where(first, pltpu.roll(x, kw - GLA_DK // 4, 1), pltpu.roll(x, GLA_DK // 4, 1))
        return x * cos + partner * sin

    q = rope(q)
    k = rope(k)
    logits = jnp.dot(lo.astype(BF16), gup_ref[...], preferred_element_type=F32) + gb_ref[...]
    la = (jnp.minimum(logits, 0.0) - jnp.log1p(jnp.exp(-jnp.abs(logits)))) / GLA_TAU

    C = GLA_CHUNK
    ri = lax.broadcasted_iota(I32, (C, C), 0)
    ci = lax.broadcasted_iota(I32, (C, C), 1)
    tri = (ri <= ci) if reverse else (ri >= ci)
    cum = jnp.where(tri, 1.0, 0.0).astype(BF16)
    tri4 = jnp.concatenate([tri] * GLA_HEADS, axis=0)
    hm_k = _head_masks(kw, GLA_DK, GLA_HEADS)
    hm_v = _head_masks(vw, GLA_DV, GLA_HEADS)
    srow = lax.broadcasted_iota(I32, (kw, vw), 0) // GLA_DK
    scol = lax.broadcasted_iota(I32, (kw, vw), 1) // GLA_DV
    blockmask = srow == scol
    ones_cv = jnp.ones((C, vw), BF16)

    S = state_ref[...]
    n_chunks = z_ref.shape[0] // C
    order = range(n_chunks - 1, -1, -1) if reverse else range(n_chunks)
    for c in order:
        sl = slice(c * C, (c + 1) * C)
        la_c = la[sl]
        b = _dot_exact(cum, la_c)
        bl = b[0:1] if reverse else b[C - 1:C]
        qs = q[sl] * jnp.exp(b)
        ks = k[sl] * jnp.exp(-b)
        ke = k[sl] * jnp.exp(bl - b)
        vb = v[sl].astype(BF16)
        qs_b = qs.astype(BF16)
        qstack = jnp.concatenate([jnp.where(hm, qs, 0.0) for hm in hm_k], axis=0).astype(BF16)
        a = lax.dot_general(qstack, ks.astype(BF16), _NT, preferred_element_type=F32)
        a = jnp.where(tri4, a, 0.0)
        o_stack = jnp.dot(a.astype(BF16), vb, preferred_element_type=F32)
        o_c = jnp.dot(qs_b, S.astype(BF16), preferred_element_type=F32)
        for h, hm in enumerate(hm_v):
            o_c = o_c + jnp.where(hm, o_stack[h * C:(h + 1) * C], 0.0)
        o_ref[sl, :] = o_c
        u = lax.dot_general(ke.astype(BF16), vb, _TN, preferred_element_type=F32)
        dcol = _dot_exact(ones_cv, la_c, dims=_TN, lhs_is_x=True)
        S = jnp.exp(dcol) * S + jnp.where(blockmask, u, 0.0)
    state_ref[...] = S

    @pl.when(g == n_steps - 1)
    def _():
        sfin_ref[0] = S


def _gla_scan(z, cos, sin, gup, gb, s0, *, B, seq, row0_blocks, tile, reverse):
    n_steps = seq // tile
    kw = GLA_HEADS * GLA_DK
    vw = GLA_HEADS * GLA_DV

    def step(g):
        return n_steps - 1 - g if reverse else g

    return pl.pallas_call(
        functools.partial(_gla_body, reverse=reverse, n_steps=n_steps), grid=(B, n_steps),
        in_specs=[
            pl.BlockSpec((tile, 1024), lambda b, g: (row0_blocks + b * n_steps + step(g), Z_GLA // 1024)),
            pl.BlockSpec((tile, kw), lambda b, g: (step(g), 0)),
            pl.BlockSpec((tile, kw), lambda b, g: (step(g), 0)),
            pl.BlockSpec((LANES, kw), lambda b, g: (0, 0)),
            pl.BlockSpec((1, kw), lambda b, g: (0, 0)),
            pl.BlockSpec((1, kw, vw), lambda b, g: (b, 0, 0)),
        ],
        out_specs=[
            pl.BlockSpec((tile, vw), lambda b, g: (b * n_steps + step(g), 0)),
            pl.BlockSpec((1, kw, vw), lambda b, g: (b, 0, 0)),
        ],
        out_shape=[jax.ShapeDtypeStruct((B * seq, vw), F32), jax.ShapeDtypeStruct((B, kw, vw), F32)],
        scratch_shapes=[pltpu.VMEM((kw, vw), F32)],
        compiler_params=_cparams(("arbitrary", "arbitrary")), name="gla_rev" if reverse else "gla_fwd",
    )(z, cos, sin, gup, gb, s0)


def _rope_tables(T):
    t = np.arange(T)
    row, col = t // GRID_W, t % GRID_W
    nf = GLA_DK // 4
    inv = 1.0 / (ROPE_BASE ** (jnp.arange(nf, dtype=F32) / nf))
    j = np.arange(GLA_HEADS * GLA_DK)
    d = j % GLA_DK
    use_col = (d // (GLA_DK // 2)) == 1
    e = d % (GLA_DK // 2)
    fi = e % nf
    pos = jnp.where(jnp.asarray(use_col)[None, :], jnp.asarray(col, F32)[:, None], jnp.asarray(row, F32)[:, None])
    ang = pos * inv[fi][None, :]
    sign = jnp.asarray(np.where(e < nf, -1.0, 1.0), F32)[None, :]
    return jnp.cos(ang), jnp.sin(ang) * sign


CONV_PAD = 16
CONV_ROWS = 128


def _conv_body(z_ref, dw_ref, b_ref, g_ref, be_ref, o_ref, ypad_ref, *, seq):
    zeros = jnp.zeros((CONV_PAD, BRANCH_W), F32)
    ypad_ref[0:CONV_PAD, :] = zeros
    ypad_ref[seq + CONV_PAD:seq + 2 * CONV_PAD, :] = zeros

    def glu(i, carry):
        base = pl.multiple_of(i * CONV_ROWS, CONV_ROWS)
        a = z_ref[pl.ds(base, CONV_ROWS), 0:BRANCH_W]
        gate = z_ref[pl.ds(base, CONV_ROWS), BRANCH_W:2 * BRANCH_W]
        ypad_ref[pl.ds(base + CONV_PAD, CONV_ROWS), :] = a * jax.nn.sigmoid(gate)
        return carry

    lax.fori_loop(0, seq // CONV_ROWS, glu, 0)

    def tile(i, carry):
        base = pl.multiple_of(i * CONV_ROWS, CONV_ROWS)
        acc = jnp.zeros((CONV_ROWS, BRANCH_W), F32)
        win = ypad_ref[pl.ds(base, CONV_ROWS + 2 * CONV_PAD), :]
        for j in range(CONV_WIDTH):
            off = CONV_PAD - CONV_WIDTH // 2 + j
            acc = acc + win[off:off + CONV_ROWS] * dw_ref[j:j + 1, :]
        y = _layer_norm(acc + b_ref[...], g_ref[...], be_ref[...])
        o_ref[pl.ds(base, CONV_ROWS), :] = y * jax.nn.sigmoid(y)
        return carry

    lax.fori_loop(0, seq // CONV_ROWS, tile, 0)


def _conformer_conv(z, dw, b, g, be, *, B, seq, row0_blocks):
    vec = pl.BlockSpec((1, BRANCH_W), lambda i: (0, 0))
    return pl.pallas_call(
        functools.partial(_conv_body, seq=seq), grid=(B,),
        in_specs=[pl.BlockSpec((seq, 2 * BRANCH_W), lambda i: (row0_blocks + i, Z_CONV // (2 * BRANCH_W))),
                  pl.BlockSpec((CONV_WIDTH, BRANCH_W), lambda i: (0, 0)), vec, vec, vec],
        out_specs=pl.BlockSpec((seq, BRANCH_W), lambda i: (i, 0)),
        out_shape=jax.ShapeDtypeStruct((B * seq, BRANCH_W), F32),
        scratch_shapes=[pltpu.VMEM((seq + 2 * CONV_PAD, BRANCH_W), F32)],
        compiler_params=_cparams(("arbitrary",)), name="conformer_conv",
    )(z, dw, b, g, be)


def _sgu_body(z_ref, g_ref, b_ref, ws_ref, bs_ref, o_ref):
    zz = jax.nn.gelu(z_ref[...])
    u = zz[:, 0:BRANCH_W]
    v = _layer_norm(zz[:, BRANCH_W:2 * BRANCH_W], g_ref[...], b_ref[...])
    gms = _head_masks(BRANCH_W, BRANCH_W // SGU_GROUPS, SGU_GROUPS)
    for c in range(z_ref.shape[0] // SGU_CHUNK):
        sl = slice(c * SGU_CHUNK, (c + 1) * SGU_CHUNK)
        vb = v[sl].astype(BF16)
        s = bs_ref[...]
        for gi, gm in enumerate(gms):
            s = s + jnp.where(gm, jnp.dot(ws_ref[gi], vb, preferred_element_type=F32), 0.0)
        o_ref[sl, :] = u[sl] * s


def _spatial_gating(z, g, b, ws, bs_exp, n_tiles):
    vec = pl.BlockSpec((1, BRANCH_W), lambda i: (0, 0))
    return pl.pallas_call(
        _sgu_body, grid=(n_tiles,),
        in_specs=[pl.BlockSpec((TOK_TILE, 2 * BRANCH_W), lambda i: (i, Z_SGU // (2 * BRANCH_W))), vec, vec,
                  pl.BlockSpec(ws.shape, lambda i: (0, 0, 0)),
                  pl.BlockSpec((SGU_CHUNK, BRANCH_W), lambda i: (0, 0))],
        out_specs=pl.BlockSpec((TOK_TILE, BRANCH_W), lambda i: (i, 0)),
        out_shape=jax.ShapeDtypeStruct((n_tiles * TOK_TILE, BRANCH_W), F32),
        compiler_params=_cparams(("arbitrary",)), name="sgu",
    )(z, g, b, ws, bs_exp)


MERGE_TILE = 256


def _merge_body(mod_ref, gates_ref, yna_ref, of_ref, ob_ref, r_ref, ycv_ref, ysg_ref, x_ref, g1_ref, sc2_ref, sh2_ref,
                ng_ref, wb_ref, wo_ref, lg_ref, lb_ref, x1_ref, h2_ref):
    o = of_ref[...] + ob_ref[...]
    vw = GLA_HEADS * GLA_DV
    hr = lax.broadcasted_iota(I32, (vw, vw), 0) // GLA_DV
    hc = lax.broadcasted_iota(I32, (vw, vw), 1) // GLA_DV
    same_head = jnp.where(hr == hc, 1.0, 0.0).astype(BF16)
    ms = _dot_exact(same_head, o * o, lhs_is_x=True) * (1.0 / GLA_DV)
    r = r_ref[...]
    y_gla = o * lax.rsqrt(ms + LN_EPS) * ng_ref[...] * (r * jax.nn.sigmoid(r))
    ys = (yna_ref[...], y_gla, ycv_ref[...], ysg_ref[...])
    merged = None
    for i in range(4):
        gate = jax.nn.sigmoid(gates_ref[:, i * D_MODEL:(i + 1) * D_MODEL])
        term = gate * jnp.dot(ys[i].astype(BF16), wb_ref[i], preferred_element_type=F32)
        merged = term if merged is None else merged + term
    y = jnp.dot(merged.astype(BF16), wo_ref[...], preferred_element_type=F32)
    x1 = _layer_norm(ALPHA * x_ref[...] + g1_ref[0] * y, lg_ref[...], lb_ref[...])
    x1_ref[...] = x1
    h2_ref[...] = x1 * (1.0 + sc2_ref[0]) + sh2_ref[0]


def _merge(z, yna, of, ob, ycv, ysg, x, g1, sc2, sh2, ng, wb, wo, lg, lb, mod_idx, n_tiles):
    d = D_MODEL
    br = pl.BlockSpec((MERGE_TILE, BRANCH_W), lambda i, m: (i, 0))
    modv = pl.BlockSpec((1, 1, d), lambda i, m: (m[i], 0, 0))
    vec = pl.BlockSpec((1, d), lambda i, m: (0, 0))
    xt = pl.BlockSpec((MERGE_TILE, d), lambda i, m: (i, 0))
    grid_spec = pltpu.PrefetchScalarGridSpec(
        num_scalar_prefetch=1, grid=(n_tiles,),
        in_specs=[
            pl.BlockSpec((MERGE_TILE, 4 * d), lambda i, m: (i, 0)),
            br, br, br,
            pl.BlockSpec((MERGE_TILE, BRANCH_W), lambda i, m: (i, (Z_GLA + 512) // BRANCH_W)),
            br, br, xt, modv, modv, modv,
            pl.BlockSpec((1, BRANCH_W), lambda i, m: (0, 0)),
            pl.BlockSpec(wb.shape, lambda i, m: (0, 0, 0)),
            pl.BlockSpec(wo.shape, lambda i, m: (0, 0)),
            vec, vec,
        ],
        out_specs=[xt, xt],
    )
    return pl.pallas_call(
        _merge_body, grid_spec=grid_spec,
        out_shape=[jax.ShapeDtypeStruct((n_tiles * MERGE_TILE, d), F32)] * 2,
        compiler_params=_cparams(("arbitrary",)), name="merge",
    )(mod_idx, z, yna, of, ob, z, ycv, ysg, x, g1, sc2, sh2, ng, wb, wo, lg, lb)


TOPK_TILE = 256


def _top16(vals, payload=None):
    n_cand = vals.shape[0]
    pos_iota = lax.broadcasted_iota(I32, vals.shape, 0).astype(F32)
    out_v, out_p = [], []
    for _ in range(PEER_TOPK):
        m = jnp.max(vals, axis=0, keepdims=True)
        pos = jnp.min(jnp.where(vals == m, pos_iota, float(n_cand)), axis=0, keepdims=True)
        hit = pos_iota == pos
        out_v.append(m)
        out_p.append(pos if payload is None else jnp.max(jnp.where(hit, payload, -1.0), axis=0, keepdims=True))
        vals = jnp.where(hit, -jnp.inf, vals)
    return jnp.concatenate(out_v, axis=0), jnp.concatenate(out_p, axis=0)


_PAIR_COUNTS = [PEER_TOPK // (i + 1) for i in range(PEER_TOPK)]


TOPK_HEADS = 2


def _topk_body(h_ref, wq_ref, keys_ref, row_ref, g_ref):
    half = PEER_DQ // 2
    n_tok = h_ref.shape[0]
    q = jnp.dot(h_ref[...].astype(BF16), wq_ref[...], preferred_element_type=F32).astype(BF16)
    for h in range(TOPK_HEADS):
        tops = []
        for s in range(2):
            qs = q[:, (2 * h + s) * half:(2 * h + s + 1) * half]
            sc = lax.dot_general(keys_ref[h, s], qs, _NT, preferred_element_type=F32)
            tops.append(_top16(sc))
        (a, ia), (b, ib) = tops
        cand = [a[i:i + 1] + b[0:n] for i, n in enumerate(_PAIR_COUNTS)]
        code = [ia[i:i + 1] * float(PEER_NKEYS) + ib[0:n] for i, n in enumerate(_PAIR_COUNTS)]
        pad = -sum(_PAIR_COUNTS) % SUBLANES
        cand = jnp.concatenate(cand + [jnp.full((pad, n_tok), -jnp.inf, F32)], axis=0)
        code = jnp.concatenate(code + [jnp.zeros((pad, n_tok), F32)], axis=0)
        best, expert = _top16(cand, code)
        e = jnp.exp(best - jnp.max(best, axis=0, keepdims=True))
        g_ref[h] = e / jnp.sum(e, axis=0, keepdims=True)
        row_ref[h] = expert.astype(I32) * HALF_SUB


def _peer_topk(h2, wq, keys, n_tiles):
    ntok = n_tiles * TOPK_TILE
    d = h2.shape[1]
    out = pl.BlockSpec((TOPK_HEADS, PEER_TOPK, TOPK_TILE), lambda i, h: (h, 0, i))
    return pl.pallas_call(
        _topk_body, grid=(n_tiles, PEER_HEADS // TOPK_HEADS),
        in_specs=[pl.BlockSpec((TOPK_TILE, d), lambda i, h: (i, 0)),
                  pl.BlockSpec((d, TOPK_HEADS * PEER_DQ), lambda i, h: (0, h)),
                  pl.BlockSpec((TOPK_HEADS, 2, PEER_NKEYS, PEER_DQ // 2), lambda i, h: (h, 0, 0, 0))],
        out_specs=[out, out],
        out_shape=[jax.ShapeDtypeStruct((PEER_HEADS, PEER_TOPK, ntok), I32),
                   jax.ShapeDtypeStruct((PEER_HEADS, PEER_TOPK, ntok), F32)],
        compiler_params=_cparams(("arbitrary", "arbitrary")), name="peer_topk",
    )(h2, wq, keys)


HALF_SUB = SUBLANES // 2
N_SEL = PEER_HEADS * PEER_TOPK
SEL_ROWS = SUBLANES * N_SEL
U_LOW_ROWS, U_HIGH_ROWS = (7, 5, 3, 1), (6, 4, 2, 0)
V_LOW_ROWS, V_HIGH_ROWS = (0, 2, 4, 6), (1, 3, 5, 7)


PACK_TILE = 512


def _pack_body(t_ref, o_ref, *, low_rows, high_rows):
    def bf16_bits(x):
        return pltpu.bitcast(x.astype(BF16).astype(F32), I32)

    for s in range(HALF_SUB):
        low = bf16_bits(t_ref[:, low_rows[s] * LANES:(low_rows[s] + 1) * LANES])
        high = bf16_bits(t_ref[:, high_rows[s] * LANES:(high_rows[s] + 1) * LANES])
        o_ref[:, s, :] = lax.shift_right_logical(low, 16) | high


def _pack_table(tab, low_rows, high_rows):
    e, d = tab.shape
    packed = pl.pallas_call(
        functools.partial(_pack_body, low_rows=low_rows, high_rows=high_rows), grid=(e // PACK_TILE,),
        in_specs=[pl.BlockSpec((PACK_TILE, d), lambda i: (i, 0))],
        out_specs=pl.BlockSpec((PACK_TILE, HALF_SUB, LANES), lambda i: (i, 0, 0)),
        out_shape=jax.ShapeDtypeStruct((e, HALF_SUB, LANES), I32),
        compiler_params=_cparams(("arbitrary",)), name="pack_table",
    )(tab)
    return packed.reshape(e * HALF_SUB, LANES)


def _stage_rows(idx_ref, tab_ref, stage_ref, p):
    for k in range(N_SEL):
        off = pl.multiple_of(idx_ref.at[k][p], HALF_SUB)
        stage_ref[k * HALF_SUB:(k + 1) * HALF_SUB, :] = tab_ref[pl.ds(off, HALF_SUB), :]


N_SLOTS = 8


def _staged_token_loop(n_tok, stage, compute):
    for s in range(N_SLOTS):
        stage(s, s)

    def group(j, carry):
        p = N_SLOTS * j
        for s in range(N_SLOTS):
            compute(p + s, s)
            stage(jnp.minimum(p + s + N_SLOTS, n_tok - 1), s)
        return carry

    lax.fori_loop(0, n_tok // N_SLOTS, group, 0)


FOLD_LAG = SUBLANES
Y_RING = 2 * FOLD_LAG


def _for_each_row_block(rows_hbm, row_bufs, sems, tb, run_block):
    g = pl.program_id(0)

    def copy(block, slot):
        return pltpu.make_async_copy(rows_hbm.at[:, pl.ds(block * tb, tb)], row_bufs[slot], sems.at[slot])

    @pl.when(g == 0)
    def _():
        copy(0, 0).start()

    copy(2 * g + 1, 1).start()
    copy(2 * g, 0).wait()
    run_block(row_bufs[0], 0)

    @pl.when(g + 1 < pl.num_programs(0))
    def _():
        copy(2 * g + 2, 0).start()

    copy(2 * g + 1, 1).wait()
    run_block(row_bufs[1], tb)


def _peer_u_body(rows_hbm, h_ref, g_ref, tab_ref, w_ref, rows0_ref, rows1_ref, sems, *scratch):
    stages, (y_ref, a_ref) = scratch[:N_SLOTS], scratch[N_SLOTS:]
    tb = h_ref.shape[0] // 2

    def fold(t):
        ring = t & (Y_RING - 1)
        pieces = []
        for v in range(SEL_ROWS // LANES):
            z = pltpu.roll(y_ref[ring, :, v * LANES:(v + 1) * LANES], LANES - SUBLANES + 1, 1, stride=1, stride_axis=0)
            pieces.append(jnp.sum(z, axis=0, keepdims=True))
        a_ref[pl.ds(t, 1), :] = jnp.concatenate(pieces, axis=1)

    def run_block(rows_ref, base):
        def compute(p, slot):
            fold(p)
            m = pltpu.bitcast(stages[slot][...], BF16)
            y_ref[(p + FOLD_LAG) & (Y_RING - 1)] = lax.dot_general(h_ref[base + p].astype(BF16), m, _NT,
                                                                   preferred_element_type=F32)

        y_ref[0:FOLD_LAG] = jnp.zeros((FOLD_LAG, SUBLANES, SEL_ROWS), F32)
        _staged_token_loop(tb, lambda p, slot: _stage_rows(rows_ref, tab_ref, stages[slot], p), compute)
        for t in range(tb, tb + FOLD_LAG):
            fold(t)

        def gate_rows(c, carry):
            lo = pl.multiple_of(c * SUBLANES, SUBLANES)
            rows = pl.ds(base + lo, SUBLANES)
            w_ref[rows, :] = g_ref[rows, :] * jax.nn.gelu(a_ref[pl.ds(FOLD_LAG + lo, SUBLANES), :])
            return carry

        lax.fori_loop(0, tb // SUBLANES, gate_rows, 0)

    _for_each_row_block(rows_hbm, (rows0_ref, rows1_ref), sems, tb, run_block)


def _peer_v_body(rows_hbm, w_ref, tab_ref, o_ref, rows0_ref, rows1_ref, sems, *stages):
    tb = o_ref.shape[0] // 2

    def run_block(rows_ref, base):
        def compute(p, slot):
            m = pltpu.bitcast(stages[slot][...], BF16)
            wrow = w_ref[pl.ds(base + p, 1), :]
            pieces = []
            for v in range(SEL_ROWS // LANES):
                wb = jnp.broadcast_to(wrow[:, v * LANES:(v + 1) * LANES], (SUBLANES, LANES))
                pieces.append(pltpu.roll(wb, 0, 1, stride=1, stride_axis=0))
            out = jnp.dot(jnp.concatenate(pieces, axis=1).astype(BF16), m, preferred_element_type=F32)
            o_ref[pl.ds(base + p, 1), :] = jnp.concatenate([out[v:v + 1] for v in range(SUBLANES)], axis=1)

        _staged_token_loop(tb, lambda p, slot: _stage_rows(rows_ref, tab_ref, stages[slot], p), compute)

    _for_each_row_block(rows_hbm, (rows0_ref, rows1_ref), sems, tb, run_block)


def _table_spec(tab):
    return pl.BlockSpec(tab.shape, lambda i: (0, 0), pipeline_mode=pl.Buffered(1))


def _row_scratch(tb):
    return [pltpu.SMEM((N_SEL, tb), I32), pltpu.SMEM((N_SEL, tb), I32), pltpu.SemaphoreType.DMA((2,))]


def _peer_u(rows, h3, g_sparse, tab, n_tiles):
    tb = GATHER_TILE
    wide = pl.BlockSpec((2 * tb, SEL_ROWS), lambda i: (i, 0))
    return pl.pallas_call(
        _peer_u_body, grid=(n_tiles // 2,),
        in_specs=[pl.BlockSpec(memory_space=pl.ANY),
                  pl.BlockSpec((2 * tb, SUBLANES, LANES), lambda i: (i, 0, 0)), wide, _table_spec(tab)],
        out_specs=wide,
        out_shape=jax.ShapeDtypeStruct((n_tiles * tb, SEL_ROWS), F32),
        scratch_shapes=_row_scratch(tb) + [pltpu.VMEM((HALF_SUB * N_SEL, LANES), I32)] * N_SLOTS
        + [pltpu.VMEM((Y_RING, SUBLANES, SEL_ROWS), F32), pltpu.VMEM((tb + FOLD_LAG, SEL_ROWS), F32)],
        compiler_params=_cparams(("arbitrary",), vmem_mb=56), name="peer_u",
    )(rows, h3, g_sparse, tab)


def _peer_v(rows, w_sparse, tab, n_tiles):
    tb = GATHER_TILE
    return pl.pallas_call(
        _peer_v_body, grid=(n_tiles // 2,),
        in_specs=[pl.BlockSpec(memory_space=pl.ANY),
                  pl.BlockSpec((2 * tb, SEL_ROWS), lambda i: (i, 0)), _table_spec(tab)],
        out_specs=pl.BlockSpec((2 * tb, SEL_ROWS), lambda i: (i, 0)),
        out_shape=jax.ShapeDtypeStruct((n_tiles * tb, SEL_ROWS), F32),
        scratch_shapes=_row_scratch(tb) + [pltpu.VMEM((HALF_SUB * N_SEL, LANES), I32)] * N_SLOTS,
        compiler_params=_cparams(("arbitrary",), vmem_mb=56), name="peer_v",
    )(rows, w_sparse, tab)


def _ln2_body(mod_ref, x_ref, f_ref, g2_ref, lg_ref, lb_ref, o_ref):
    o_ref[...] = _layer_norm(ALPHA * x_ref[...] + g2_ref[0] * f_ref[...], lg_ref[...], lb_ref[...])


def _ln2(x1, ffn, g2, lg, lb, mod_idx, n_tiles):
    d = D_MODEL
    xt = pl.BlockSpec((TOK_TILE, d), lambda i, m: (i, 0))
    vec = pl.BlockSpec((1, d), lambda i, m: (0, 0))
    grid_spec = pltpu.PrefetchScalarGridSpec(
        num_scalar_prefetch=1, grid=(n_tiles,),
        in_specs=[xt, xt, pl.BlockSpec((1, 1, d), lambda i, m: (m[i], 0, 0)), vec, vec],
        out_specs=xt,
    )
    return pl.pallas_call(
        _ln2_body, grid_spec=grid_spec,
        out_shape=jax.ShapeDtypeStruct((n_tiles * TOK_TILE, d), F32),
        compiler_params=_cparams(("arbitrary",)), name="ln2",
    )(mod_idx, x1, ffn, g2, lg, lb)


def _mod_index(B, T, L, tile, n_tiles):
    start = np.arange(n_tiles) * tile
    return jnp.asarray(np.where(start < B * T, start // T, B), I32)


def _pad_w_in(w_in):
    d = w_in.shape[0]
    na, gla, conv, sgu, gates = (w_in[:, 0:768], w_in[:, 768:1568], w_in[:, 1568:2080],
                                 w_in[:, 2080:2592], w_in[:, 2592:6688])
    z224 = jnp.zeros((d, 1024 - 800), w_in.dtype)
    z256 = jnp.zeros((d, 1024 - 768), w_in.dtype)
    return jnp.concatenate([gates, conv, sgu, gla, z224, na, z256], axis=1).astype(BF16)


def kernel(x, c, ctx, c_ctx, ada_w, ada_b, w_in, na_rpb, gla_gate_up, gla_gate_b, gla_norm_g, conv_dw, conv_b,
           conv_ln_g, conv_ln_b, sgu_ln_g, sgu_ln_b, sgu_ws, sgu_bs, w_branch, w_out, ln1_g, ln1_b, peer_wq,
           peer_keys, peer_u, peer_v, ln2_g, ln2_b):
    B, T, D = x.shape
    L = ctx.shape[1]
    depth = ada_w.shape[0]
    n_lat, n_ctx = B * T, B * L
    ntok = n_lat + n_ctx
    kw, vw = GLA_HEADS * GLA_DK, GLA_HEADS * GLA_DV

    xa = jnp.concatenate([x.reshape(n_lat, D), ctx.reshape(n_ctx, D)], axis=0)
    cos_l, sin_l = _rope_tables(T)
    cos_c, sin_c = jnp.ones((L, kw), F32), jnp.zeros((L, kw), F32)
    s_zero = jnp.zeros((B, kw, vw), F32)
    n_mod = -(-(B + 1) // SUBLANES) * SUBLANES
    cpad = jnp.concatenate([c, c_ctx[None, :], jnp.zeros((n_mod - B - 1, D), F32)], axis=0)

    for l in range(depth):
        need_ctx = l < depth - 1
        n_act = ntok if need_ctx else n_lat
        mod = _ada(cpad, ada_w[l], ada_b[l][None, :])
        sh1, sc1, g1, sh2, sc2, g2 = [m[:, None, :] for m in jnp.split(mod, 6, axis=-1)]

        z = _modmm(xa, sc1, sh1, _pad_w_in(w_in[l]), _mod_index(B, T, L, TOK_TILE, ntok // TOK_TILE),
                   ntok // TOK_TILE, 1024)

        y_na = _na_attention(z, _na_bias_table(na_rpb[l]), B, T, L)

        gups = []
        for d in range(2):
            gu = jnp.zeros((LANES, kw), F32).at[d * GLA_GATE_RANK:(d + 1) * GLA_GATE_RANK].set(gla_gate_up[l, d])
            gups.append(gu.astype(BF16))
        gbs = [gla_gate_b[l, d][None, :] for d in range(2)]
        ctx_blocks = n_lat // L
        oc_f, sc_f = _gla_scan(z, cos_c, sin_c, gups[0], gbs[0], s_zero, B=B, seq=L, row0_blocks=ctx_blocks,
                               tile=L, reverse=False)
        oc_b, sc_b = _gla_scan(z, cos_c, sin_c, gups[1], gbs[1], s_zero, B=B, seq=L, row0_blocks=ctx_blocks,
                               tile=L, reverse=True)
        o_f, _ = _gla_scan(z, cos_l, sin_l, gups[0], gbs[0], sc_f, B=B, seq=T, row0_blocks=0, tile=TOK_TILE,
                           reverse=False)
        o_b, _ = _gla_scan(z, cos_l, sin_l, gups[1], gbs[1], sc_b, B=B, seq=T, row0_blocks=0, tile=TOK_TILE,
                           reverse=True)

        cv_args = (conv_dw[l], conv_b[l][None, :], conv_ln_g[l][None, :], conv_ln_b[l][None, :])
        y_cv = _conformer_conv(z, *cv_args, B=B, seq=T, row0_blocks=0)
        bs_exp = jnp.repeat(sgu_bs[l].T, BRANCH_W // SGU_GROUPS, axis=1)
        y_sg = _spatial_gating(z, sgu_ln_g[l][None, :], sgu_ln_b[l][None, :], sgu_ws[l].astype(BF16), bs_exp,
                               n_act // TOK_TILE)

        if need_ctx:
            y_na = jnp.concatenate([y_na, _ctx_attention(z, B, T, L)], axis=0)
            o_f = jnp.concatenate([o_f, oc_f], axis=0)
            o_b = jnp.concatenate([o_b, oc_b], axis=0)
            y_cv = jnp.concatenate([y_cv, _conformer_conv(z, *cv_args, B=B, seq=L, row0_blocks=ctx_blocks)], axis=0)

        x1, h2 = _merge(z, y_na, o_f, o_b, y_cv, y_sg, xa, g1, sc2, sh2, gla_norm_g[l].reshape(1, vw),
                        w_branch[l].astype(BF16), w_out[l].astype(BF16), ln1_g[l][None, :], ln1_b[l][None, :],
                        _mod_index(B, T, L, MERGE_TILE, n_act // MERGE_TILE), n_act // MERGE_TILE)

        idx_t, g_t = _peer_topk(h2, peer_wq[l].astype(BF16), peer_keys[l].astype(BF16), n_act // TOPK_TILE)
        idx = idx_t.reshape(N_SEL, n_act)
        gate = g_t.reshape(N_SEL, n_act).T
        g_sparse = jnp.pad(gate[:, :, None], ((0, 0), (0, 0), (0, SUBLANES - 1))).reshape(n_act, SEL_ROWS)
        w_sparse = _peer_u(idx, h2.reshape(n_act, SUBLANES, LANES), g_sparse,
                           _pack_table(peer_u[l], U_LOW_ROWS, U_HIGH_ROWS), n_act // GATHER_TILE)
        ffn = _peer_v(idx, w_sparse, _pack_table(peer_v[l], V_LOW_ROWS, V_HIGH_ROWS), n_act // GATHER_TILE)

        xa = _ln2(x1, ffn, g2, ln2_g[l][None, :], ln2_b[l][None, :],
                  _mod_index(B, T, L, TOK_TILE, n_act // TOK_TILE), n_act // TOK_TILE)

    return xa[:n_lat].reshape(B, T, D)
```
